```python
import jax, jax.numpy as jnp
from jax import lax
import numpy as np

D_MODEL = 2048
BATCH = 8
SEQ = 2048
DEPTH = 2
DEC_BATCH = 8
DEC_SEQ = 4096
PAST_LEN = 128

HEAD_DIM = 128
N_Q_HEADS = 8
N_KV_HEADS = 2
Q_PER_KV = N_Q_HEADS // N_KV_HEADS
ATTN_WIDTH = N_Q_HEADS * HEAD_DIM
KV_WIDTH = N_KV_HEADS * HEAD_DIM
SGU_GROUPS = 8
SGU_GROUP_DIM = 128
SGU_WIDTH = SGU_GROUPS * SGU_GROUP_DIM
CHUNK = 128
N_MEM = 256
MEM_HEADS = 4
MEM_WIDTH = MEM_HEADS * HEAD_DIM
N_BRANCH = 3
MIX_WIDTH = ATTN_WIDTH + SGU_WIDTH + MEM_WIDTH
IN_SIZES = (ATTN_WIDTH, KV_WIDTH, KV_WIDTH, SGU_WIDTH, SGU_WIDTH, MEM_WIDTH, N_BRANCH * D_MODEL)
IN_WIDTH = sum(IN_SIZES)
N_EXPERTS = 16
EXPERT_FF = 2 * D_MODEL
EC_FACTOR = 2
GRID_W = 64
Q_BLOCK = 128
ROPE_THETA = 10000.0
ROPE_FREQS = HEAD_DIM // 4
EPS = 1e-6
DN_ALPHA = (2 * DEPTH) ** 0.25
DN_BETA = (8 * DEPTH) ** -0.25

kernel_name = "hybrid_gqa_sgu_mem_ec_moe_encoder"


def layer_norm(x, g, b):
    x32 = x.astype(jnp.float32)
    mu = jnp.mean(x32, axis=-1, keepdims=True)
    var = jnp.mean(jnp.square(x32 - mu), axis=-1, keepdims=True)
    y = (x32 - mu) * lax.rsqrt(var + EPS) * g.astype(jnp.float32) + b.astype(jnp.float32)
    return y.astype(x.dtype)


def rms_norm_f32(x, g):
    x32 = x.astype(jnp.float32)
    return x32 * lax.rsqrt(jnp.mean(jnp.square(x32), axis=-1, keepdims=True) + EPS) * g.astype(jnp.float32)


def axial_rope_tables(seq):
    rows = seq // GRID_W
    row = jnp.repeat(jnp.arange(rows), GRID_W).astype(jnp.float32)
    col = (jnp.arange(rows * GRID_W) % GRID_W).astype(jnp.float32)
    inv = ROPE_THETA ** (-jnp.arange(ROPE_FREQS, dtype=jnp.float32) / ROPE_FREQS)
    ang = jnp.stack([row[:, None] * inv, col[:, None] * inv], axis=1)
    return jnp.cos(ang), jnp.sin(ang)


def apply_axial_rope(x, cos, sin):
    xs = x.reshape(x.shape[:-1] + (2, 2, ROPE_FREQS))
    x1 = xs[..., 0, :]
    x2 = xs[..., 1, :]
    c = cos[None, :, None]
    s = sin[None, :, None]
    out = jnp.stack([x1 * c - x2 * s, x2 * c + x1 * s], axis=-2)
    return out.reshape(x.shape)


def gqa_attention(q, k, v):
    bsz, seq = q.shape[0], q.shape[1]
    nb = seq // Q_BLOCK
    qg = (q * (HEAD_DIM ** -0.5)).reshape(bsz, nb, Q_BLOCK, N_KV_HEADS, Q_PER_KV, HEAD_DIM)
    qg = jnp.moveaxis(qg, 1, 0)

    def block(qb):
        s = jnp.einsum('bqkgd,bskd->bkgqs', qb, k)
        p = jax.nn.softmax(s, axis=-1).astype(v.dtype)
        return jnp.einsum('bkgqs,bskd->bqkgd', p, v)

    o = lax.map(block, qg)
    return jnp.moveaxis(o, 0, 1).reshape(bsz, seq, ATTN_WIDTH)


def spatial_gating(u, v, ln_g, ln_b, w_s, b_s):
    bsz, seq = u.shape[0], u.shape[1]
    vn = layer_norm(v, ln_g, ln_b)
    vc = vn.reshape(bsz, seq // CHUNK, CHUNK, SGU_GROUPS, SGU_GROUP_DIM)
    mixed = jnp.einsum('gpq,bnqgc->bnpgc', w_s, vc) + b_s.T[None, None, :, :, None]
    return u * mixed.reshape(bsz, seq, SGU_WIDTH)


def memory_attention(q, mem, w_mem_kv):
    bsz, seq = q.shape[0], q.shape[1]
    kv = mem @ w_mem_kv
    k = kv[..., :MEM_WIDTH].reshape(bsz, N_MEM, MEM_HEADS, HEAD_DIM)
    v = kv[..., MEM_WIDTH:].reshape(bsz, N_MEM, MEM_HEADS, HEAD_DIM)
    qh = q.reshape(bsz, seq, MEM_HEADS, HEAD_DIM)
    s = jnp.einsum('bshd,bmhd->bhsm', qh, k, preferred_element_type=jnp.float32) * (HEAD_DIM ** -0.5)
    p = jax.nn.softmax(s, axis=-1).astype(v.dtype)
    return jnp.einsum('bhsm,bmhd->bshd', p, v).reshape(bsz, seq, MEM_WIDTH)


def expert_choice_ffn(x, w_router, w_gate, w_up, w_down):
    bsz, seq, d = x.shape
    n_tok = bsz * seq
    cap = EC_FACTOR * n_tok // N_EXPERTS
    xf = x.reshape(n_tok, d)
    aff = jax.nn.softmax((xf @ w_router).astype(jnp.float32), axis=-1)
    gval, gidx = lax.top_k(aff.T, cap)

    def expert(args):
        idx, g, wg, wu, wd = args
        xe = xf[idx]
        h = jax.nn.silu(xe @ wg) * (xe @ wu)
        return (h @ wd) * g[:, None].astype(x.dtype)

    ye = lax.map(expert, (gidx, gval, w_gate, w_up, w_down))
    out = jnp.zeros_like(xf).at[gidx.reshape(-1)].add(ye.reshape(-1, d))
    return out.reshape(bsz, seq, d)


def encoder_layer(x, mem, cos, sin, w_in, b_gate, q_norm_g, k_norm_g, sgu_ln_g, sgu_ln_b, w_s, b_s,
                  w_mem_kv, w_branch, w_o, ln1_g, ln1_b, w_router, w_gate, w_up, w_down, ln2_g, ln2_b):
    bsz, seq, d = x.shape
    h = x @ w_in
    pts = []
    acc = 0
    for sz in IN_SIZES[:-1]:
        acc += sz
        pts.append(acc)
    q, k, v, u_b, v_b, q_m, g_logit = jnp.split(h, pts, axis=-1)

    qa = apply_axial_rope(rms_norm_f32(q.reshape(bsz, seq, N_Q_HEADS, HEAD_DIM), q_norm_g), cos, sin)
    ka = apply_axial_rope(rms_norm_f32(k.reshape(bsz, seq, N_KV_HEADS, HEAD_DIM), k_norm_g), cos, sin)
    va = v.reshape(bsz, seq, N_KV_HEADS, HEAD_DIM)
    out_a = gqa_attention(qa, ka, va)

    out_b = spatial_gating(jax.nn.gelu(u_b), jax.nn.gelu(v_b), sgu_ln_g, sgu_ln_b, w_s, b_s)

    out_c = memory_attention(q_m, mem, w_mem_kv)

    gates = jax.nn.sigmoid((g_logit + b_gate).astype(jnp.float32)).astype(x.dtype).reshape(bsz, seq, N_BRANCH, d)
    p_a = out_a @ w_branch[:ATTN_WIDTH]
    p_b = out_b @ w_branch[ATTN_WIDTH:ATTN_WIDTH + SGU_WIDTH]
    p_c = out_c @ w_branch[ATTN_WIDTH + SGU_WIDTH:]
    merged = gates[:, :, 0] * p_a + gates[:, :, 1] * p_b + gates[:, :, 2] * p_c
    x = layer_norm(DN_ALPHA * x + merged @ w_o, ln1_g, ln1_b)

    x = layer_norm(DN_ALPHA * x + expert_choice_ffn(x, w_router, w_gate, w_up, w_down), ln2_g, ln2_b)
    return x


def run_trunk(x, mem, w_in, b_gate, q_norm_g, k_norm_g, sgu_ln_g, sgu_ln_b, w_s, b_s, w_mem_kv, w_branch,
              w_o, ln1_g, ln1_b, w_router, w_gate, w_up, w_down, ln2_g, ln2_b):
    cos, sin = axial_rope_tables(x.shape[1])
    for l in range(DEPTH):
        x = encoder_layer(x, mem, cos, sin, w_in[l], b_gate[l], q_norm_g[l], k_norm_g[l], sgu_ln_g[l],
                          sgu_ln_b[l], w_s[l], b_s[l], w_mem_kv[l], w_branch[l], w_o[l], ln1_g[l], ln1_b[l],
                          w_router[l], w_gate[l], w_up[l], w_down[l], ln2_g[l], ln2_b[l])
    return x


def setup_inputs(seed: int = 0) -> dict:
    key = jax.random.key(seed)
    ks = jax.random.split(key, 24)
    f32 = jnp.float32
    nrm = lambda k, shape, scale: jax.random.normal(k, shape, f32) * scale
    return {
        "x_prompt": nrm(ks[0], (BATCH, SEQ, D_MODEL), 1.0),
        "x_sample": nrm(ks[1], (DEC_BATCH, DEC_SEQ, D_MODEL), 1.0),
        "mem_prompt": nrm(ks[2], (BATCH, N_MEM, D_MODEL), 1.0),
        "mem_sample": nrm(ks[3], (DEC_BATCH, N_MEM, D_MODEL), 1.0),
        "w_in": nrm(ks[4], (DEPTH, D_MODEL, IN_WIDTH), D_MODEL ** -0.5),
        "b_gate": nrm(ks[5], (DEPTH, N_BRANCH * D_MODEL), 0.02),
        "q_norm_g": 1.0 + nrm(ks[6], (DEPTH, HEAD_DIM), 0.02),
        "k_norm_g": 1.0 + nrm(ks[7], (DEPTH, HEAD_DIM), 0.02),
        "sgu_ln_g": 1.0 + nrm(ks[8], (DEPTH, SGU_WIDTH), 0.02),
        "sgu_ln_b": nrm(ks[9], (DEPTH, SGU_WIDTH), 0.02),
        "w_s": nrm(ks[10], (DEPTH, SGU_GROUPS, CHUNK, CHUNK), 0.5 * CHUNK ** -0.5),
        "b_s": 1.0 + nrm(ks[11], (DEPTH, SGU_GROUPS, CHUNK), 0.02),
        "w_mem_kv": nrm(ks[12], (DEPTH, D_MODEL, 2 * MEM_WIDTH), D_MODEL ** -0.5),
        "w_branch": nrm(ks[13], (DEPTH, MIX_WIDTH, D_MODEL), ATTN_WIDTH ** -0.5),
        "w_o": nrm(ks[14], (DEPTH, D_MODEL, D_MODEL), DN_BETA * D_MODEL ** -0.5),
        "ln1_g": 1.0 + nrm(ks[15], (DEPTH, D_MODEL), 0.02),
        "ln1_b": nrm(ks[16], (DEPTH, D_MODEL), 0.02),
        "w_router": nrm(ks[17], (DEPTH, D_MODEL, N_EXPERTS), D_MODEL ** -0.5),
        "w_gate": nrm(ks[18], (DEPTH, N_EXPERTS, D_MODEL, EXPERT_FF), D_MODEL ** -0.5),
        "w_up": nrm(ks[19], (DEPTH, N_EXPERTS, D_MODEL, EXPERT_FF), D_MODEL ** -0.5),
        "w_down": nrm(ks[20], (DEPTH, N_EXPERTS, EXPERT_FF, D_MODEL), DN_BETA * EXPERT_FF ** -0.5),
        "ln2_g": 1.0 + nrm(ks[21], (DEPTH, D_MODEL), 0.02),
        "ln2_b": nrm(ks[22], (DEPTH, D_MODEL), 0.02),
    }


def reference(x_prompt, x_sample, mem_prompt, mem_sample, w_in, b_gate, q_norm_g, k_norm_g, sgu_ln_g,
              sgu_ln_b, w_s, b_s, w_mem_kv, w_branch, w_o, ln1_g, ln1_b, w_router, w_gate, w_up, w_down,
              ln2_g, ln2_b):
    y_prompt = run_trunk(x_prompt, mem_prompt, w_in, b_gate, q_norm_g, k_norm_g, sgu_ln_g, sgu_ln_b, w_s,
                         b_s, w_mem_kv, w_branch, w_o, ln1_g, ln1_b, w_router, w_gate, w_up, w_down,
                         ln2_g, ln2_b)
    y_sample = run_trunk(x_sample, mem_sample, w_in, b_gate, q_norm_g, k_norm_g, sgu_ln_g, sgu_ln_b, w_s,
                         b_s, w_mem_kv, w_branch, w_o, ln1_g, ln1_b, w_router, w_gate, w_up, w_down,
                         ln2_g, ln2_b)
    return (y_prompt, y_sample)
```

```python
import functools

import jax
import jax.numpy as jnp
from jax import lax
from jax.experimental import pallas as pl
from jax.experimental.pallas import tpu as pltpu

F32 = jnp.float32
BF16 = jnp.bfloat16

HEAD_DIM = 128
N_Q_HEADS = 8
N_KV_HEADS = 2
Q_PER_KV = N_Q_HEADS // N_KV_HEADS
ATTN_WIDTH = N_Q_HEADS * HEAD_DIM
KV_WIDTH = N_KV_HEADS * HEAD_DIM
SGU_GROUPS = 8
SGU_WIDTH = SGU_GROUPS * 128
CHUNK = 128
MEM_HEADS = 4
MEM_WIDTH = MEM_HEADS * HEAD_DIM
N_BRANCH = 3
N_EXPERTS = 16
EC_FACTOR = 2
GRID_W = 64
ROPE_THETA = 10000.0
ROPE_FREQS = HEAD_DIM // 4
EPS = 1e-6
LANES = 128
VMEM_LIMIT = 56 * 1024 * 1024


def _cparams(sem, vmem=VMEM_LIMIT):
    return pltpu.CompilerParams(dimension_semantics=sem, vmem_limit_bytes=vmem)


def _gelu_tanh(x):
    return 0.5 * x * (1.0 + jnp.tanh(0.7978845608028654 * (x + 0.044715 * (x * x * x))))


def _sigmoid(x):
    return 1.0 / (1.0 + jnp.exp(-x))


def _layer_norm_rows(x, g, b):
    mu = jnp.mean(x, axis=-1, keepdims=True)
    xc = x - mu
    var = jnp.mean(xc * xc, axis=-1, keepdims=True)
    return xc * lax.rsqrt(var + EPS) * g + b


def _mm_plain_kernel(x_ref, w_ref, o_ref):
    o_ref[...] = jnp.dot(x_ref[...], w_ref[...], preferred_element_type=F32).astype(o_ref.dtype)


def _mm_plain(x, w, tm):
    m, k = x.shape
    n = w.shape[1]
    return pl.pallas_call(
        _mm_plain_kernel,
        grid=(m // tm,),
        in_specs=[pl.BlockSpec((tm, k), lambda i: (i, 0)),
                  pl.BlockSpec((k, n), lambda i: (0, 0))],
        out_specs=pl.BlockSpec((tm, n), lambda i: (i, 0)),
        out_shape=jax.ShapeDtypeStruct((m, n), BF16),
        compiler_params=_cparams(("parallel",)),
        name="mm_plain",
    )(x, w)


def _proj_attn_kernel(x_ref, w_ref, cos_ref, sin_ref, gq_ref, gk_ref, q_ref, k_ref, v_ref, qm_ref):
    acc = jnp.dot(x_ref[...], w_ref[...], preferred_element_type=F32)
    cos = cos_ref[...]
    sin = sin_ref[...]
    lane = lax.broadcasted_iota(jnp.int32, cos.shape, 1)
    first_half = (lane % (2 * ROPE_FREQS)) < ROPE_FREQS
    scale = HEAD_DIM ** -0.5

    def norm_rope(blk, g):
        ms = jnp.mean(blk * blk, axis=-1, keepdims=True)
        y = blk * lax.rsqrt(ms + EPS) * g
        partner = jnp.where(first_half,
                            pltpu.roll(y, HEAD_DIM - ROPE_FREQS, 1),
                            pltpu.roll(y, ROPE_FREQS, 1))
        return y * cos + partner * sin

    gq = gq_ref[...]
    gk = gk_ref[...]
    for h in range(N_Q_HEADS):
        blk = acc[:, h * HEAD_DIM:(h + 1) * HEAD_DIM]
        q_ref[:, h * HEAD_DIM:(h + 1) * HEAD_DIM] = (norm_rope(blk, gq) * scale).astype(q_ref.dtype)
    for h in range(N_KV_HEADS):
        c0 = ATTN_WIDTH + h * HEAD_DIM
        k_ref[:, h * HEAD_DIM:(h + 1) * HEAD_DIM] = norm_rope(acc[:, c0:c0 + HEAD_DIM], gk).astype(k_ref.dtype)
    c0 = ATTN_WIDTH + KV_WIDTH
    v_ref[...] = acc[:, c0:c0 + KV_WIDTH].astype(v_ref.dtype)
    c0 += KV_WIDTH
    qm_ref[...] = (acc[:, c0:c0 + MEM_WIDTH] * scale).astype(qm_ref.dtype)


def _proj_attn(x_bf, w_attn, cos_t, sin_t, gq, gk, seq, tm):
    m, d = x_bf.shape
    n = w_attn.shape[1]
    per_seq = seq // tm
    row = lambda i: (i, 0)
    const = lambda i: (0, 0)
    return pl.pallas_call(
        _proj_attn_kernel,
        grid=(m // tm,),
        in_specs=[pl.BlockSpec((tm, d), row),
                  pl.BlockSpec((d, n), const),
                  pl.BlockSpec((tm, HEAD_DIM), lambda i: (i % per_seq, 0)),
                  pl.BlockSpec((tm, HEAD_DIM), lambda i: (i % per_seq, 0)),
                  pl.BlockSpec((1, HEAD_DIM), const),
                  pl.BlockSpec((1, HEAD_DIM), const)],
        out_specs=[pl.BlockSpec((tm, ATTN_WIDTH), row),
                   pl.BlockSpec((tm, KV_WIDTH), row),
                   pl.BlockSpec((tm, KV_WIDTH), row),
                   pl.BlockSpec((tm, MEM_WIDTH), row)],
        out_shape=[jax.ShapeDtypeStruct((m, ATTN_WIDTH), BF16),
                   jax.ShapeDtypeStruct((m, KV_WIDTH), BF16),
                   jax.ShapeDtypeStruct((m, KV_WIDTH), BF16),
                   jax.ShapeDtypeStruct((m, MEM_WIDTH), BF16)],
        compiler_params=_cparams(("parallel",)),
        name="proj_attn",
    )(x_bf, w_attn, cos_t, sin_t, gq, gk)


def _proj_sgu_kernel(x_ref, w_ref, lng_ref, lnb_ref, ws_ref, bs_ref, o_ref):
    acc = jnp.dot(x_ref[...], w_ref[...], preferred_element_type=F32)
    u = _gelu_tanh(acc[:, :SGU_WIDTH])
    v = _gelu_tanh(acc[:, SGU_WIDTH:])
    vn = _layer_norm_rows(v, lng_ref[...], lnb_ref[...]).astype(BF16)
    tm = acc.shape[0]
    for c in range(tm // CHUNK):
        r0 = c * CHUNK
        for g in range(SGU_GROUPS):
            c0 = g * 128
            mixed = jnp.dot(ws_ref[g], vn[r0:r0 + CHUNK, c0:c0 + 128], preferred_element_type=F32)
            mixed = mixed + bs_ref[g]
            o_ref[r0:r0 + CHUNK, c0:c0 + 128] = (u[r0:r0 + CHUNK, c0:c0 + 128] * mixed).astype(o_ref.dtype)


def _proj_sgu(x_bf, w_uv, ln_g, ln_b, w_s, b_s, tm):
    m, d = x_bf.shape
    return pl.pallas_call(
        _proj_sgu_kernel,
        grid=(m // tm,),
        in_specs=[pl.BlockSpec((tm, d), lambda i: (i, 0)),
                  pl.BlockSpec((d, 2 * SGU_WIDTH), lambda i: (0, 0)),
                  pl.BlockSpec((1, SGU_WIDTH), lambda i: (0, 0)),
                  pl.BlockSpec((1, SGU_WIDTH), lambda i: (0, 0)),
                  pl.BlockSpec((SGU_GROUPS, CHUNK, CHUNK), lambda i: (0, 0, 0)),
                  pl.BlockSpec((SGU_GROUPS, CHUNK, 1), lambda i: (0, 0, 0))],
        out_specs=pl.BlockSpec((tm, SGU_WIDTH), lambda i: (i, 0)),
        out_shape=jax.ShapeDtypeStruct((m, SGU_WIDTH), BF16),
        compiler_params=_cparams(("parallel",)),
        name="proj_sgu",
    )(x_bf, w_uv, ln_g, ln_b, w_s, b_s)


def _softmax_pv(s, v):
    mx = jnp.max(s, axis=-1, keepdims=True)
    p = jnp.exp(s - mx)
    l = jnp.sum(p, axis=-1, keepdims=True)
    o = jnp.dot(p.astype(BF16), v, preferred_element_type=F32)
    return o / l


_NT = (((1,), (1,)), ((), ()))


def _gqa_kernel(q_ref, k_ref, v_ref, o_ref):
    k = k_ref[...]
    v = v_ref[...]
    for h in range(Q_PER_KV):
        q = q_ref[:, h * HEAD_DIM:(h + 1) * HEAD_DIM]
        s = lax.dot_general(q, k, _NT, preferred_element_type=F32)
        o_ref[:, h * HEAD_DIM:(h + 1) * HEAD_DIM] = _softmax_pv(s, v).astype(o_ref.dtype)


def _gqa(q, k, v, bsz, seq, tq):
    m = q.shape[0]
    nq = seq // tq
    gw = Q_PER_KV * HEAD_DIM
    return pl.pallas_call(
        _gqa_kernel,
        grid=(bsz, N_KV_HEADS, nq),
        in_specs=[pl.BlockSpec((tq, gw), lambda b, g, i: (b * nq + i, g)),
                  pl.BlockSpec((seq, HEAD_DIM), lambda b, g, i: (b, g)),
                  pl.BlockSpec((seq, HEAD_DIM), lambda b, g, i: (b, g))],
        out_specs=pl.BlockSpec((tq, gw), lambda b, g, i: (b * nq + i, g)),
        out_shape=jax.ShapeDtypeStruct((m, ATTN_WIDTH), BF16),
        compiler_params=_cparams(("parallel", "parallel", "parallel")),
        name="gqa",
    )(q, k, v)


def _mem_attn_kernel(q_ref, kv_ref, o_ref):
    for h in range(MEM_HEADS):
        q = q_ref[:, h * HEAD_DIM:(h + 1) * HEAD_DIM]
        k = kv_ref[:, h * HEAD_DIM:(h + 1) * HEAD_DIM]
        v = kv_ref[:, MEM_WIDTH + h * HEAD_DIM:MEM_WIDTH + (h + 1) * HEAD_DIM]
        s = lax.dot_general(q, k, _NT, preferred_element_type=F32)
        o_ref[:, h * HEAD_DIM:(h + 1) * HEAD_DIM] = _softmax_pv(s, v).astype(o_ref.dtype)


def _mem_attn(qm, kv, bsz, seq, n_mem, tq):
    m = qm.shape[0]
    nq = seq // tq
    return pl.pallas_call(
        _mem_attn_kernel,
        grid=(bsz, nq),
        in_specs=[pl.BlockSpec((tq, MEM_WIDTH), lambda b, i: (b * nq + i, 0)),
                  pl.BlockSpec((n_mem, 2 * MEM_WIDTH), lambda b, i: (b, 0))],
        out_specs=pl.BlockSpec((tq, MEM_WIDTH), lambda b, i: (b * nq + i, 0)),
        out_shape=jax.ShapeDtypeStruct((m, MEM_WIDTH), BF16),
        compiler_params=_cparams(("parallel", "parallel")),
        name="mem_attn",
    )(qm, kv)


def _branch_merge_kernel(x_ref, wg0, wg1, wg2, bg0, bg1, bg2, oa_ref, ob_ref, oc_ref,
                         wba, wbb, wbc, o_ref):
    x = x_ref[...]
    acc = None
    for wg, bg, o, wb in ((wg0, bg0, oa_ref, wba), (wg1, bg1, ob_ref, wbb), (wg2, bg2, oc_ref, wbc)):
        gate = _sigmoid(jnp.dot(x, wg[...], preferred_element_type=F32) + bg[...])
        p = jnp.dot(o[...], wb[...], preferred_element_type=F32)
        acc = gate * p if acc is None else acc + gate * p
    o_ref[...] = acc.astype(o_ref.dtype)


def _branch_merge(x_bf, w_g, b_g, oa, ob, oc, w_branch, tm, tn):
    m, d = x_bf.shape
    nj = d // tn
    row = lambda j, i: (i, 0)
    in_specs = [pl.BlockSpec((tm, d), row)]
    in_specs += [pl.BlockSpec((d, tn), functools.partial(lambda j, i, br: (0, br * nj + j), br=br))
                 for br in range(N_BRANCH)]
    in_specs += [pl.BlockSpec((1, tn), functools.partial(lambda j, i, br: (0, br * nj + j), br=br))
                 for br in range(N_BRANCH)]
    in_specs += [pl.BlockSpec((tm, ATTN_WIDTH), row),
                 pl.BlockSpec((tm, SGU_WIDTH), row),
                 pl.BlockSpec((tm, MEM_WIDTH), row),
                 pl.BlockSpec((ATTN_WIDTH, tn), lambda j, i: (0, j)),
                 pl.BlockSpec((SGU_WIDTH, tn), lambda j, i: (ATTN_WIDTH // SGU_WIDTH, j)),
                 pl.BlockSpec((MEM_WIDTH, tn), lambda j, i: ((ATTN_WIDTH + SGU_WIDTH) // MEM_WIDTH, j))]
    return pl.pallas_call(
        _branch_merge_kernel,
        grid=(nj, m // tm),
        in_specs=in_specs,
        out_specs=pl.BlockSpec((tm, tn), lambda j, i: (i, j)),
        out_shape=jax.ShapeDtypeStruct((m, d), BF16),
        compiler_params=_cparams(("parallel", "parallel")),
        name="branch_merge",
    )(x_bf, w_g, w_g, w_g, b_g, b_g, b_g, oa, ob, oc, w_branch, w_branch, w_branch)


def _split_bf16(x):
    hi = x.astype(BF16)
    lo = (x - hi.astype(F32)).astype(BF16)
    return hi, lo


def _out_ln1_kernel(alpha, mg_ref, wo_ref, x_ref, g_ref, b_ref, wrh_ref, wrl_ref,
                    x1e_ref, xs_ref, afft_ref):
    d = x_ref.shape[1]
    y = jnp.dot(mg_ref[...], wo_ref[...], preferred_element_type=F32)
    x1 = _layer_norm_rows(alpha * x_ref[...] + y, g_ref[...], b_ref[...])
    x1e_ref[:, :d] = x1
    xs_ref[...] = alpha * x1
    xh, xl = _split_bf16(x1)
    logits = (jnp.dot(xh, wrh_ref[...], preferred_element_type=F32)
              + jnp.dot(xh, wrl_ref[...], preferred_element_type=F32)
              + jnp.dot(xl, wrh_ref[...], preferred_element_type=F32))
    lane = lax.broadcasted_iota(jnp.int32, logits.shape, 1)
    valid = lane < N_EXPERTS
    logits = jnp.where(valid, logits, -1e30)
    mx = jnp.max(logits, axis=-1, keepdims=True)
    e = jnp.where(valid, jnp.exp(logits - mx), 0.0)
    aff = e / jnp.sum(e, axis=-1, keepdims=True)
    x1e_ref[:, d:] = aff
    afft_ref[...] = aff.T[:N_EXPERTS, :]


def _out_ln1(merged, w_o, x, ln_g, ln_b, wr_hi, wr_lo, alpha, tm):
    m, d = x.shape
    row = lambda i: (i, 0)
    const = lambda i: (0, 0)
    return pl.pallas_call(
        functools.partial(_out_ln1_kernel, alpha),
        grid=(m // tm,),
        in_specs=[pl.BlockSpec((tm, d), row),
                  pl.BlockSpec((d, d), const),
                  pl.BlockSpec((tm, d), row),
                  pl.BlockSpec((1, d), const),
                  pl.BlockSpec((1, d), const),
                  pl.BlockSpec((d, LANES), const),
                  pl.BlockSpec((d, LANES), const)],
        out_specs=[pl.BlockSpec((tm, d + LANES), row),
                   pl.BlockSpec((tm, d), row),
                   pl.BlockSpec((N_EXPERTS, tm), lambda i: (0, i))],
        out_shape=[jax.ShapeDtypeStruct((m, d + LANES), F32),
                   jax.ShapeDtypeStruct((m, d), F32),
                   jax.ShapeDtypeStruct((N_EXPERTS, m), F32)],
        compiler_params=_cparams(("parallel",)),
        name="out_ln1",
    )(merged, w_o, x, ln_g, ln_b, wr_hi, wr_lo)


_BISECT_STEPS = 40


def _route_kernel(cap, a_ref, idx_ref):
    a = a_ref[...]
    n_e, n_r, _ = a.shape
    capf = float(cap)

    def count(mask):
        part = jnp.sum(jnp.where(mask, 1.0, 0.0), axis=2, keepdims=True)
        return jnp.sum(part, axis=1, keepdims=True)

    def bisect(_, carry):
        lo, hi = carry
        mid = 0.5 * (lo + hi)
        ok = count(a >= mid) >= capf
        return jnp.where(ok, mid, lo), jnp.where(ok, hi, mid)

    lo0 = jnp.zeros((n_e, 1, 1), F32)
    hi0 = jnp.full((n_e, 1, 1), 2.0, F32)
    _, hi = lax.fori_loop(0, _BISECT_STEPS, bisect, (lo0, hi0))

    def below_max(hi):
        part = jnp.max(jnp.where(a < hi, a, -1.0), axis=2, keepdims=True)
        return jnp.max(part, axis=1, keepdims=True)

    def refine_cond(carry):
        hi, t = carry
        short = jnp.where(count(a >= t) < capf, 1.0, 0.0)
        return jnp.max(short) > 0.0

    def refine_body(carry):
        hi, t = carry
        hi = jnp.where(count(a >= t) < capf, t, hi)
        return hi, below_max(hi)

    _, thr = lax.while_loop(refine_cond, refine_body, (hi, below_max(hi)))

    ri = lax.broadcasted_iota(jnp.int32, (LANES, LANES), 0)
    ci = lax.broadcasted_iota(jnp.int32, (LANES, LANES), 1)
    tri = jnp.where(ri <= ci, 1.0, 0.0).astype(BF16)
    rr = lax.broadcasted_iota(jnp.int32, (n_r, n_r), 0)
    rc = lax.broadcasted_iota(jnp.int32, (n_r, n_r), 1)
    lstrict = jnp.where(rc < rr, 1.0, 0.0).astype(BF16)

    def prefix_incl(mask_f):
        within = jnp.dot(mask_f.reshape(n_e * n_r, LANES).astype(BF16), tri,
                         preferred_element_type=F32).reshape(n_e, n_r, LANES)
        outs = []
        for e in range(n_e):
            off = jnp.dot(lstrict, within[e].astype(BF16), preferred_element_type=F32)
            outs.append(within[e] + off[:, LANES - 1:LANES])
        return outs

    gt = a > thr
    eq = a == thr
    need = capf - count(gt)
    eq_f = jnp.where(eq, 1.0, 0.0)
    eq_incl = prefix_incl(eq_f)
    j_row = lax.broadcasted_iota(jnp.int32, (1, cap), 1).astype(F32)
    r_col = lax.broadcasted_iota(jnp.int32, (n_r, 1), 0).astype(F32)
    sel_list = []
    for e in range(n_e):
        eq_excl = eq_incl[e] - eq_f[e]
        sel_list.append(jnp.where(gt[e] | (eq[e] & (eq_excl < need[e])), 1.0, 0.0))
    slot_incl = prefix_incl(jnp.stack(sel_list, axis=0))
    for e in range(n_e):
        s_e = slot_incl[e]
        c_e = s_e[:, LANES - 1:LANES]
        blk = jnp.sum(jnp.where(c_e <= j_row, 1.0, 0.0), axis=0, keepdims=True)
        onehot_t = jnp.where(r_col == blk, 1.0, 0.0).astype(BF16)
        s_hi = jnp.floor(s_e * (1.0 / 64.0))
        s_lo = s_e - 64.0 * s_hi
        rows_t = (64.0 * jnp.dot(s_hi.T.astype(BF16), onehot_t, preferred_element_type=F32)
                  + jnp.dot(s_lo.T.astype(BF16), onehot_t, preferred_element_type=F32))
        within = jnp.sum(jnp.where(rows_t <= j_row, 1.0, 0.0), axis=0, keepdims=True)
        idx_ref[e:e + 1, :] = (blk * float(LANES) + within).astype(jnp.int32)


def _route(aff3, cap):
    n_e, n_r, _ = aff3.shape
    return pl.pallas_call(
        functools.partial(_route_kernel, cap),
        grid=(1,),
        in_specs=[pl.BlockSpec((n_e, n_r, LANES), lambda i: (0, 0, 0))],
        out_specs=pl.BlockSpec((n_e, cap), lambda i: (0, 0)),
        out_shape=jax.ShapeDtypeStruct((n_e, cap), jnp.int32),
        compiler_params=_cparams(("arbitrary",)),
        name="route",
    )(aff3)


def _row_copy(src_hbm, dst_vmem, tok, r, sem):
    return pltpu.make_async_copy(src_hbm.at[pl.ds(tok, 1), :], dst_vmem.at[pl.ds(r, 1), :], sem)


def _row_copy_back(src_vmem, dst_hbm, tok, r, sem):
    return pltpu.make_async_copy(src_vmem.at[pl.ds(r, 1), :], dst_hbm.at[pl.ds(tok, 1), :], sem)


def _gather_start(idx_ref, src_hbm, buf, sem, n_rows):
    def body(r, c):
        _row_copy(src_hbm, buf, idx_ref[0, 0, r], r, sem).start()
        return c
    lax.fori_loop(0, n_rows, body, 0, unroll=8)


def _gather_wait(idx_ref, src_hbm, buf, sem, n_rows):
    def body(r, c):
        _row_copy(src_hbm, buf, idx_ref[0, 0, r], r, sem).wait()
        return c
    lax.fori_loop(0, n_rows, body, 0, unroll=8)


def _scatter_start(idx_ref, buf, dst_hbm, sem, n_rows):
    def body(r, c):
        _row_copy_back(buf, dst_hbm, idx_ref[0, 0, r], r, sem).start()
        return c
    lax.fori_loop(0, n_rows, body, 0, unroll=8)


def _scatter_wait(idx_ref, buf, dst_hbm, sem, n_rows):
    def body(r, c):
        _row_copy_back(buf, dst_hbm, idx_ref[0, 0, r], r, sem).wait()
        return c
    lax.fori_loop(0, n_rows, body, 0, unroll=8)


def _moe_gather_kernel(tiles_per_expert, d, idx_ref, x1e_hbm, xe_ref, g_ref, buf, sem):
    tm = buf.shape[0]
    _gather_start(idx_ref, x1e_hbm, buf, sem, tm)
    _gather_wait(idx_ref, x1e_hbm, buf, sem, tm)
    e = pl.program_id(0) // tiles_per_expert
    xe_ref[...] = buf[:, :d].astype(xe_ref.dtype)
    aff = buf[:, d:]
    lane = lax.broadcasted_iota(jnp.int32, aff.shape, 1)
    g = jnp.sum(jnp.where(lane == e, aff, 0.0), axis=-1, keepdims=True)
    g_ref[...] = jnp.broadcast_to(g, g_ref.shape)


def _moe_gather(idx3, x1e, cap, tm):
    n_tiles = idx3.shape[0]
    d = x1e.shape[1] - LANES
    rows = n_tiles * tm
    return pl.pallas_call(
        functools.partial(_moe_gather_kernel, cap // tm, d),
        grid=(n_tiles,),
        in_specs=[pl.BlockSpec((1, 1, tm), lambda i: (i, 0, 0), memory_space=pltpu.SMEM),
                  pl.BlockSpec(memory_space=pl.ANY)],
        out_specs=[pl.BlockSpec((tm, d), lambda i: (i, 0)),
                   pl.BlockSpec((tm, LANES), lambda i: (i, 0))],
        out_shape=[jax.ShapeDtypeStruct((rows, d), BF16),
                   jax.ShapeDtypeStruct((rows, LANES), F32)],
        scratch_shapes=[pltpu.VMEM((tm, d + LANES), F32), pltpu.SemaphoreType.DMA(())],
        compiler_params=_cparams(("arbitrary",)),
        name="moe_gather",
    )(idx3, x1e)


def _moe_up_kernel(x_ref, wg_ref, wu_ref, h_ref):
    x = x_ref[...]
    a = jnp.dot(x, wg_ref[...], preferred_element_type=F32)
    b = jnp.dot(x, wu_ref[...], preferred_element_type=F32)
    h_ref[...] = (a * _sigmoid(a) * b).astype(h_ref.dtype)


def _moe_up(xe, w_gate, w_up, cap, tm, tf):
    rows, d = xe.shape
    n_e, _, ff = w_gate.shape
    per_e = cap // tm
    return pl.pallas_call(
        _moe_up_kernel,
        grid=(n_e, ff // tf, per_e),
        in_specs=[pl.BlockSpec((tm, d), lambda e, f, i: (e * per_e + i, 0)),
                  pl.BlockSpec((None, d, tf), lambda e, f, i: (e, 0, f)),
                  pl.BlockSpec((None, d, tf), lambda e, f, i: (e, 0, f))],
        out_specs=pl.BlockSpec((tm, tf), lambda e, f, i: (e * per_e + i, f)),
        out_shape=jax.ShapeDtypeStruct((rows, ff), BF16),
        compiler_params=_cparams(("parallel", "parallel", "parallel")),
        name="moe_up",
    )(xe, w_gate, w_up)


def _moe_down_kernel(idx_ref, h_ref, wd_ref, g_ref, acc_in_hbm, acc_hbm, buf, gsem, ssem):
    del acc_in_hbm
    tm = buf.shape[0]
    _gather_start(idx_ref, acc_hbm, buf, gsem, tm)
    ye = jnp.dot(h_ref[...], wd_ref[...], preferred_element_type=F32) * g_ref[:, 0:1]
    _gather_wait(idx_ref, acc_hbm, buf, gsem, tm)
    buf[...] = buf[...] + ye
    _scatter_start(idx_ref, buf, acc_hbm, ssem, tm)
    _scatter_wait(idx_ref, buf, acc_hbm, ssem, tm)


def _moe_down(idx3, h, w_down, g, acc, cap, tm):
    rows, ff = h.shape
    n_e, _, d = w_down.shape
    per_e = cap // tm
    return pl.pallas_call(
        _moe_down_kernel,
        grid=(n_e, per_e),
        in_specs=[pl.BlockSpec((1, 1, tm), lambda e, i: (e * per_e + i, 0, 0), memory_space=pltpu.SMEM),
                  pl.BlockSpec((tm, ff), lambda e, i: (e * per_e + i, 0)),
                  pl.BlockSpec((None, ff, d), lambda e, i: (e, 0, 0)),
                  pl.BlockSpec((tm, LANES), lambda e, i: (e * per_e + i, 0)),
                  pl.BlockSpec(memory_space=pl.ANY)],
        out_specs=pl.BlockSpec(memory_space=pl.ANY),
        out_shape=jax.ShapeDtypeStruct(acc.shape, acc.dtype),
        scratch_shapes=[pltpu.VMEM((tm, d), F32), pltpu.SemaphoreType.DMA(()), pltpu.SemaphoreType.DMA(())],
        input_output_aliases={4: 0},
        compiler_params=_cparams(("arbitrary", "arbitrary")),
        name="moe_down",
    )(idx3, h, w_down, g, acc)


def _ln2_kernel(x_ref, g_ref, b_ref, y_ref, ybf_ref):
    y = _layer_norm_rows(x_ref[...], g_ref[...], b_ref[...])
    y_ref[...] = y
    ybf_ref[...] = y.astype(ybf_ref.dtype)


def _ln2(x, g, b, tm):
    m, d = x.shape
    row = lambda i: (i, 0)
    return pl.pallas_call(
        _ln2_kernel,
        grid=(m // tm,),
        in_specs=[pl.BlockSpec((tm, d), row),
                  pl.BlockSpec((1, d), lambda i: (0, 0)),
                  pl.BlockSpec((1, d), lambda i: (0, 0))],
        out_specs=[pl.BlockSpec((tm, d), row), pl.BlockSpec((tm, d), row)],
        out_shape=[jax.ShapeDtypeStruct((m, d), F32), jax.ShapeDtypeStruct((m, d), BF16)],
        compiler_params=_cparams(("parallel",)),
        name="ln2",
    )(x, g, b)


def _rope_tables(seq):
    rows = seq // GRID_W
    row = jnp.repeat(jnp.arange(rows), GRID_W).astype(F32)
    col = (jnp.arange(rows * GRID_W) % GRID_W).astype(F32)
    inv = ROPE_THETA ** (-jnp.arange(ROPE_FREQS, dtype=F32) / ROPE_FREQS)
    ang_r = row[:, None] * inv
    ang_c = col[:, None] * inv
    cos_t = jnp.concatenate([jnp.cos(ang_r), jnp.cos(ang_r), jnp.cos(ang_c), jnp.cos(ang_c)], axis=1)
    sin_t = jnp.concatenate([-jnp.sin(ang_r), jnp.sin(ang_r), -jnp.sin(ang_c), jnp.sin(ang_c)], axis=1)
    return cos_t, sin_t


def _prep_layer(l, w_in, b_gate, q_norm_g, k_norm_g, sgu_ln_g, sgu_ln_b, w_s, b_s, w_mem_kv, w_branch,
                w_o, ln1_g, ln1_b, w_router, w_gate, w_up, w_down, ln2_g, ln2_b):
    d = w_in.shape[1]
    wi = w_in[l]
    c_q, c_k, c_v = ATTN_WIDTH, ATTN_WIDTH + KV_WIDTH, ATTN_WIDTH + 2 * KV_WIDTH
    c_u = c_v + SGU_WIDTH
    c_vb = c_u + SGU_WIDTH
    c_qm = c_vb + MEM_WIDTH
    wr = jnp.pad(w_router[l], ((0, 0), (0, LANES - N_EXPERTS)))
    wr_hi = wr.astype(BF16)
    wr_lo = (wr - wr_hi.astype(F32)).astype(BF16)
    return dict(
        w_attn=jnp.concatenate([wi[:, :c_v], wi[:, c_vb:c_qm]], axis=1).astype(BF16),
        w_uv=wi[:, c_v:c_vb].astype(BF16),
        w_g=wi[:, c_qm:].astype(BF16),
        b_g=b_gate[l].reshape(1, N_BRANCH * d),
        gq=q_norm_g[l].reshape(1, HEAD_DIM), gk=k_norm_g[l].reshape(1, HEAD_DIM),
        sgu_g=sgu_ln_g[l].reshape(1, SGU_WIDTH), sgu_b=sgu_ln_b[l].reshape(1, SGU_WIDTH),
        w_s=w_s[l].astype(BF16), b_s=b_s[l].reshape(SGU_GROUPS, CHUNK, 1),
        w_mem_kv=w_mem_kv[l].astype(BF16), w_branch=w_branch[l].astype(BF16), w_o=w_o[l].astype(BF16),
        ln1_g=ln1_g[l].reshape(1, d), ln1_b=ln1_b[l].reshape(1, d),
        wr_hi=wr_hi, wr_lo=wr_lo,
        w_gate=w_gate[l].astype(BF16), w_up=w_up[l].astype(BF16), w_down=w_down[l].astype(BF16),
        ln2_g=ln2_g[l].reshape(1, d), ln2_b=ln2_b[l].reshape(1, d),
    )


def _run_trunk(x, mem, layers, alpha):
    bsz, seq, d = x.shape
    n_mem = mem.shape[1]
    m = bsz * seq
    cap = EC_FACTOR * m // N_EXPERTS
    cos_t, sin_t = _rope_tables(seq)
    xf = x.reshape(m, d)
    x_bf = xf.astype(BF16)
    mem_bf = mem.reshape(bsz * n_mem, d).astype(BF16)
    tm_moe = min(512, cap)
    for p in layers:
        q, k, v, qm = _proj_attn(x_bf, p["w_attn"], cos_t, sin_t, p["gq"], p["gk"], seq, tm=512)
        out_b = _proj_sgu(x_bf, p["w_uv"], p["sgu_g"], p["sgu_b"], p["w_s"], p["b_s"], tm=512)
        out_a = _gqa(q, k, v, bsz, seq, tq=256)
        kv = _mm_plain(mem_bf, p["w_mem_kv"], tm=min(512, bsz * n_mem))
        out_c = _mem_attn(qm, kv, bsz, seq, n_mem, tq=512)
        merged = _branch_merge(x_bf, p["w_g"], p["b_g"], out_a, out_b, out_c, p["w_branch"], tm=512, tn=512)
        x1e, acc, aff_t = _out_ln1(merged, p["w_o"], xf, p["ln1_g"], p["ln1_b"], p["wr_hi"], p["wr_lo"],
                                   alpha, tm=512)
        idx = _route(aff_t.reshape(N_EXPERTS, m // LANES, LANES), cap)
        idx3 = idx.reshape(N_EXPERTS * cap // tm_moe, 1, tm_moe)
        xe, g = _moe_gather(idx3, x1e, cap, tm_moe)
        h = _moe_up(xe, p["w_gate"], p["w_up"], cap, tm=tm_moe, tf=min(1024, p["w_gate"].shape[2]))
        acc = _moe_down(idx3, h, p["w_down"], g, acc, cap, tm_moe)
        xf, x_bf = _ln2(acc, p["ln2_g"], p["ln2_b"], tm=512)
    return xf.reshape(bsz, seq, d)


def kernel(x_prompt, x_sample, mem_prompt, mem_sample, w_in, b_gate, q_norm_g, k_norm_g, sgu_ln_g, sgu_ln_b,
           w_s, b_s, w_mem_kv, w_branch, w_o, ln1_g, ln1_b, w_router, w_gate, w_up, w_down, ln2_g, ln2_b):
    depth = w_in.shape[0]
    alpha = (2 * depth) ** 0.25
    layers = [_prep_layer(l, w_in, b_gate, q_norm_g, k_norm_g, sgu_ln_g, sgu_ln_b, w_s, b_s, w_mem_kv,
                          w_branch, w_o, ln1_g, ln1_b, w_router, w_gate, w_up, w_down, ln2_g, ln2_b)
              for l in range(depth)]
    y_prompt = _run_trunk(x_prompt, mem_prompt, layers, alpha)
    y_sample = _run_trunk(x_sample, mem_sample, layers, alpha)
    return (y_prompt, y_sample)
```

```python
import functools

import jax
import jax.numpy as jnp
from jax import lax
from jax.experimental import pallas as pl
from jax.experimental.pallas import tpu as pltpu

F32 = jnp.float32
BF16 = jnp.bfloat16

HEAD_DIM = 128
N_Q_HEADS = 8
N_KV_HEADS = 2
Q_PER_KV = N_Q_HEADS // N_KV_HEADS
ATTN_WIDTH = N_Q_HEADS * HEAD_DIM
KV_WIDTH = N_KV_HEADS * HEAD_DIM
SGU_GROUPS = 8
SGU_WIDTH = SGU_GROUPS * 128
CHUNK = 128
MEM_HEADS = 4
MEM_WIDTH = MEM_HEADS * HEAD_DIM
N_BRANCH = 3
N_EXPERTS = 16
EC_FACTOR = 2
GRID_W = 64
ROPE_THETA = 10000.0
ROPE_FREQS = HEAD_DIM // 4
EPS = 1e-6
LANES = 128
VMEM_LIMIT = 56 * 1024 * 1024


def _cparams(sem, vmem=VMEM_LIMIT):
    return pltpu.CompilerParams(dimension_semantics=sem, vmem_limit_bytes=vmem)


def _gelu_tanh(x):
    return 0.5 * x * (1.0 + jnp.tanh(0.7978845608028654 * (x + 0.044715 * (x * x * x))))


def _sigmoid(x):
    return 1.0 / (1.0 + jnp.exp(-x))


def _layer_norm_rows(x, g, b):
    mu = jnp.mean(x, axis=-1, keepdims=True)
    xc = x - mu
    var = jnp.mean(xc * xc, axis=-1, keepdims=True)
    return xc * lax.rsqrt(var + EPS) * g + b


def _mm_plain_kernel(x_ref, w_ref, o_ref):
    o_ref[...] = jnp.dot(x_ref[...], w_ref[...], preferred_element_type=F32).astype(o_ref.dtype)


def _mm_plain(x, w, tm):
    m, k = x.shape
    n = w.shape[1]
    return pl.pallas_call(
        _mm_plain_kernel,
        grid=(m // tm,),
        in_specs=[pl.BlockSpec((tm, k), lambda i: (i, 0)),
                  pl.BlockSpec((k, n), lambda i: (0, 0))],
        out_specs=pl.BlockSpec((tm, n), lambda i: (i, 0)),
        out_shape=jax.ShapeDtypeStruct((m, n), BF16),
        compiler_params=_cparams(("parallel",)),
        name="mm_plain",
    )(x, w)


def _proj_attn_kernel(x_ref, w_ref, cos_ref, sin_ref, gq_ref, gk_ref, q_ref, k_ref, v_ref, qm_ref):
    acc = jnp.dot(x_ref[...], w_ref[...], preferred_element_type=F32)
    cos = cos_ref[...]
    sin = sin_ref[...]
    lane = lax.broadcasted_iota(jnp.int32, cos.shape, 1)
    first_half = (lane % (2 * ROPE_FREQS)) < ROPE_FREQS
    scale = HEAD_DIM ** -0.5

    def norm_rope(blk, g):
        ms = jnp.mean(blk * blk, axis=-1, keepdims=True)
        y = blk * lax.rsqrt(ms + EPS) * g
        partner = jnp.where(first_half,
                            pltpu.roll(y, HEAD_DIM - ROPE_FREQS, 1),
                            pltpu.roll(y, ROPE_FREQS, 1))
        return y * cos + partner * sin

    gq = gq_ref[...]
    gk = gk_ref[...]
    for h in range(N_Q_HEADS):
        blk = acc[:, h * HEAD_DIM:(h + 1) * HEAD_DIM]
        q_ref[:, h * HEAD_DIM:(h + 1) * HEAD_DIM] = (norm_rope(blk, gq) * scale).astype(q_ref.dtype)
    for h in range(N_KV_HEADS):
        c0 = ATTN_WIDTH + h * HEAD_DIM
        k_ref[:, h * HEAD_DIM:(h + 1) * HEAD_DIM] = norm_rope(acc[:, c0:c0 + HEAD_DIM], gk).astype(k_ref.dtype)
    c0 = ATTN_WIDTH + KV_WIDTH
    v_ref[...] = acc[:, c0:c0 + KV_WIDTH].astype(v_ref.dtype)
    c0 += KV_WIDTH
    qm_ref[...] = (acc[:, c0:c0 + MEM_WIDTH] * scale).astype(qm_ref.dtype)


def _proj_attn(x_bf, w_attn, cos_t, sin_t, gq, gk, seq, tm):
    m, d = x_bf.shape
    n = w_attn.shape[1]
    per_seq = seq // tm
    row = lambda i: (i, 0)
    const = lambda i: (0, 0)
    return pl.pallas_call(
        _proj_attn_kernel,
        grid=(m // tm,),
        in_specs=[pl.BlockSpec((tm, d), row),
                  pl.BlockSpec((d, n), const),
                  pl.BlockSpec((tm, HEAD_DIM), lambda i: (i % per_seq, 0)),
                  pl.BlockSpec((tm, HEAD_DIM), lambda i: (i % per_seq, 0)),
                  pl.BlockSpec((1, HEAD_DIM), const),
                  pl.BlockSpec((1, HEAD_DIM), const)],
        out_specs=[pl.BlockSpec((tm, ATTN_WIDTH), row),
                   pl.BlockSpec((tm, KV_WIDTH), row),
                   pl.BlockSpec((tm, KV_WIDTH), row),
                   pl.BlockSpec((tm, MEM_WIDTH), row)],
        out_shape=[jax.ShapeDtypeStruct((m, ATTN_WIDTH), BF16),
                   jax.ShapeDtypeStruct((m, KV_WIDTH), BF16),
                   jax.ShapeDtypeStruct((m, KV_WIDTH), BF16),
                   jax.ShapeDtypeStruct((m, MEM_WIDTH), BF16)],
        compiler_params=_cparams(("parallel",)),
        name="proj_attn",
    )(x_bf, w_attn, cos_t, sin_t, gq, gk)


def _proj_sgu_kernel(x_ref, w_ref, lng_ref, lnb_ref, ws_ref, bs_ref, o_ref):
    acc = jnp.dot(x_ref[...], w_ref[...], preferred_element_type=F32)
    u = _gelu_tanh(acc[:, :SGU_WIDTH])
    v = _gelu_tanh(acc[:, SGU_WIDTH:])
    vn = _layer_norm_rows(v, lng_ref[...], lnb_ref[...]).astype(BF16)
    tm = acc.shape[0]
    for c in range(tm // CHUNK):
        r0 = c * CHUNK
        for g in range(SGU_GROUPS):
            c0 = g * 128
            mixed = jnp.dot(ws_ref[g], vn[r0:r0 + CHUNK, c0:c0 + 128], preferred_element_type=F32)
            mixed = mixed + bs_ref[g]
            o_ref[r0:r0 + CHUNK, c0:c0 + 128] = (u[r0:r0 + CHUNK, c0:c0 + 128] * mixed).astype(o_ref.dtype)


def _proj_sgu(x_bf, w_uv, ln_g, ln_b, w_s, b_s, tm):
    m, d = x_bf.shape
    return pl.pallas_call(
        _proj_sgu_kernel,
        grid=(m // tm,),
        in_specs=[pl.BlockSpec((tm, d), lambda i: (i, 0)),
                  pl.BlockSpec((d, 2 * SGU_WIDTH), lambda i: (0, 0)),
                  pl.BlockSpec((1, SGU_WIDTH), lambda i: (0, 0)),
                  pl.BlockSpec((1, SGU_WIDTH), lambda i: (0, 0)),
                  pl.BlockSpec((SGU_GROUPS, CHUNK, CHUNK), lambda i: (0, 0, 0)),
                  pl.BlockSpec((SGU_GROUPS, CHUNK, 1), lambda i: (0, 0, 0))],
        out_specs=pl.BlockSpec((tm, SGU_WIDTH), lambda i: (i, 0)),
        out_shape=jax.ShapeDtypeStruct((m, SGU_WIDTH), BF16),
        compiler_params=_cparams(("parallel",)),
        name="proj_sgu",
    )(x_bf, w_uv, ln_g, ln_b, w_s, b_s)


def _softmax_pv(s, v):
    mx = jnp.max(s, axis=-1, keepdims=True)
    p = jnp.exp(s - mx)
    l = jnp.sum(p, axis=-1, keepdims=True)
    o = jnp.dot(p.astype(BF16), v, preferred_element_type=F32)
    return o / l


_NT = (((1,), (1,)), ((), ()))


def _gqa_kernel(q_ref, k_ref, v_ref, o_ref):
    k = k_ref[...]
    v = v_ref[...]
    for h in range(Q_PER_KV):
        q = q_ref[:, h * HEAD_DIM:(h + 1) * HEAD_DIM]
        s = lax.dot_general(q, k, _NT, preferred_element_type=F32)
        o_ref[:, h * HEAD_DIM:(h + 1) * HEAD_DIM] = _softmax_pv(s, v).astype(o_ref.dtype)


def _gqa(q, k, v, bsz, seq, tq):
    m = q.shape[0]
    nq = seq // tq
    gw = Q_PER_KV * HEAD_DIM
    return pl.pallas_call(
        _gqa_kernel,
        grid=(bsz, N_KV_HEADS, nq),
        in_specs=[pl.BlockSpec((tq, gw), lambda b, g, i: (b * nq + i, g)),
                  pl.BlockSpec((seq, HEAD_DIM), lambda b, g, i: (b, g)),
                  pl.BlockSpec((seq, HEAD_DIM), lambda b, g, i: (b, g))],
        out_specs=pl.BlockSpec((tq, gw), lambda b, g, i: (b * nq + i, g)),
        out_shape=jax.ShapeDtypeStruct((m, ATTN_WIDTH), BF16),
        compiler_params=_cparams(("parallel", "parallel", "parallel")),
        name="gqa",
    )(q, k, v)


def _mem_attn_kernel(q_ref, kv_ref, o_ref):
    for h in range(MEM_HEADS):
        q = q_ref[:, h * HEAD_DIM:(h + 1) * HEAD_DIM]
        k = kv_ref[:, h * HEAD_DIM:(h + 1) * HEAD_DIM]
        v = kv_ref[:, MEM_WIDTH + h * HEAD_DIM:MEM_WIDTH + (h + 1) * HEAD_DIM]
        s = lax.dot_general(q, k, _NT, preferred_element_type=F32)
        o_ref[:, h * HEAD_DIM:(h + 1) * HEAD_DIM] = _softmax_pv(s, v).astype(o_ref.dtype)


def _mem_attn(qm, kv, bsz, seq, n_mem, tq):
    m = qm.shape[0]
    nq = seq // tq
    return pl.pallas_call(
        _mem_attn_kernel,
        grid=(bsz, nq),
        in_specs=[pl.BlockSpec((tq, MEM_WIDTH), lambda b, i: (b * nq + i, 0)),
                  pl.BlockSpec((n_mem, 2 * MEM_WIDTH), lambda b, i: (b, 0))],
        out_specs=pl.BlockSpec((tq, MEM_WIDTH), lambda b, i: (b * nq + i, 0)),
        out_shape=jax.ShapeDtypeStruct((m, MEM_WIDTH), BF16),
        compiler_params=_cparams(("parallel", "parallel")),
        name="mem_attn",
    )(qm, kv)


def _branch_merge_kernel(x_ref, wg0, wg1, wg2, bg0, bg1, bg2, oa_ref, ob_ref, oc_ref,
                         wba, wbb, wbc, o_ref):
    x = x_ref[...]
    acc = None
    for wg, bg, o, wb in ((wg0, bg0, oa_ref, wba), (wg1, bg1, ob_ref, wbb), (wg2, bg2, oc_ref, wbc)):
        gate = _sigmoid(jnp.dot(x, wg[...], preferred_element_type=F32) + bg[...])
        p = jnp.dot(o[...], wb[...], preferred_element_type=F32)
        acc = gate * p if acc is None else acc + gate * p
    o_ref[...] = acc.astype(o_ref.dtype)


def _branch_merge(x_bf, w_g, b_g, oa, ob, oc, w_branch, tm, tn):
    m, d = x_bf.shape
    nj = d // tn
    row = lambda j, i: (i, 0)
    in_specs = [pl.BlockSpec((tm, d), row)]
    in_specs += [pl.BlockSpec((d, tn), functools.partial(lambda j, i, br: (0, br * nj + j), br=br))
                 for br in range(N_BRANCH)]
    in_specs += [pl.BlockSpec((1, tn), functools.partial(lambda j, i, br: (0, br * nj + j), br=br))
                 for br in range(N_BRANCH)]
    in_specs += [pl.BlockSpec((tm, ATTN_WIDTH), row),
                 pl.BlockSpec((tm, SGU_WIDTH), row),
                 pl.BlockSpec((tm, MEM_WIDTH), row),
                 pl.BlockSpec((ATTN_WIDTH, tn), lambda j, i: (0, j)),
                 pl.BlockSpec((SGU_WIDTH, tn), lambda j, i: (ATTN_WIDTH // SGU_WIDTH, j)),
                 pl.BlockSpec((MEM_WIDTH, tn), lambda j, i: ((ATTN_WIDTH + SGU_WIDTH) // MEM_WIDTH, j))]
    return pl.pallas_call(
        _branch_merge_kernel,
        grid=(nj, m // tm),
        in_specs=in_specs,
        out_specs=pl.BlockSpec((tm, tn), lambda j, i: (i, j)),
        out_shape=jax.ShapeDtypeStruct((m, d), BF16),
        compiler_params=_cparams(("parallel", "parallel")),
        name="branch_merge",
    )(x_bf, w_g, w_g, w_g, b_g, b_g, b_g, oa, ob, oc, w_branch, w_branch, w_branch)


def _split_bf16(x):
    hi = x.astype(BF16)
    lo = (x - hi.astype(F32)).astype(BF16)
    return hi, lo


def _out_ln1_kernel(alpha, mg_ref, wo_ref, x_ref, g_ref, b_ref, wrh_ref, wrl_ref,
                    x1e_ref, xs_ref, afft_ref):
    d = x_ref.shape[1]
    y = jnp.dot(mg_ref[...], wo_ref[...], preferred_element_type=F32)
    x1 = _layer_norm_rows(alpha * x_ref[...] + y, g_ref[...], b_ref[...])
    x1e_ref[:, :d] = x1
    xs_ref[0] = alpha * x1
    xs_ref[1] = jnp.zeros_like(x1)
    xh, xl = _split_bf16(x1)
    logits = (jnp.dot(xh, wrh_ref[...], preferred_element_type=F32)
              + jnp.dot(xh, wrl_ref[...], preferred_element_type=F32)
              + jnp.dot(xl, wrh_ref[...], preferred_element_type=F32))
    lane = lax.broadcasted_iota(jnp.int32, logits.shape, 1)
    valid = lane < N_EXPERTS
    logits = jnp.where(valid, logits, -1e30)
    mx = jnp.max(logits, axis=-1, keepdims=True)
    e = jnp.where(valid, jnp.exp(logits - mx), 0.0)
    aff = e / jnp.sum(e, axis=-1, keepdims=True)
    x1e_ref[:, d:] = aff
    afft_ref[...] = aff.T[:N_EXPERTS, :]


def _out_ln1(merged, w_o, x, ln_g, ln_b, wr_hi, wr_lo, alpha, tm):
    m, d = x.shape
    row = lambda i: (i, 0)
    const = lambda i: (0, 0)
    return pl.pallas_call(
        functools.partial(_out_ln1_kernel, alpha),
        grid=(m // tm,),
        in_specs=[pl.BlockSpec((tm, d), row),
                  pl.BlockSpec((d, d), const),
                  pl.BlockSpec((tm, d), row),
                  pl.BlockSpec((1, d), const),
                  pl.BlockSpec((1, d), const),
                  pl.BlockSpec((d, LANES), const),
                  pl.BlockSpec((d, LANES), const)],
        out_specs=[pl.BlockSpec((tm, d + LANES), row),
                   pl.BlockSpec((2, tm, d), lambda i: (0, i, 0)),
                   pl.BlockSpec((N_EXPERTS, tm), lambda i: (0, i))],
        out_shape=[jax.ShapeDtypeStruct((m, d + LANES), F32),
                   jax.ShapeDtypeStruct((2, m, d), F32),
                   jax.ShapeDtypeStruct((N_EXPERTS, m), F32)],
        compiler_params=_cparams(("parallel",)),
        name="out_ln1",
    )(merged, w_o, x, ln_g, ln_b, wr_hi, wr_lo)


_BISECT_STEPS = 40


def _route_kernel(cap, a_ref, idx_ref):
    a = a_ref[...]
    n_e, n_r, _ = a.shape
    capf = float(cap)

    def count(mask):
        part = jnp.sum(jnp.where(mask, 1.0, 0.0), axis=2, keepdims=True)
        return jnp.sum(part, axis=1, keepdims=True)

    def bisect(_, carry):
        lo, hi = carry
        mid = 0.5 * (lo + hi)
        ok = count(a >= mid) >= capf
        return jnp.where(ok, mid, lo), jnp.where(ok, hi, mid)

    lo0 = jnp.zeros((n_e, 1, 1), F32)
    hi0 = jnp.full((n_e, 1, 1), 2.0, F32)
    _, hi = lax.fori_loop(0, _BISECT_STEPS, bisect, (lo0, hi0))

    def below_max(hi):
        part = jnp.max(jnp.where(a < hi, a, -1.0), axis=2, keepdims=True)
        return jnp.max(part, axis=1, keepdims=True)

    def refine_cond(carry):
        hi, t = carry
        short = jnp.where(count(a >= t) < capf, 1.0, 0.0)
        return jnp.max(short) > 0.0

    def refine_body(carry):
        hi, t = carry
        hi = jnp.where(count(a >= t) < capf, t, hi)
        return hi, below_max(hi)

    _, thr = lax.while_loop(refine_cond, refine_body, (hi, below_max(hi)))

    ri = lax.broadcasted_iota(jnp.int32, (LANES, LANES), 0)
    ci = lax.broadcasted_iota(jnp.int32, (LANES, LANES), 1)
    tri = jnp.where(ri <= ci, 1.0, 0.0).astype(BF16)
    rr = lax.broadcasted_iota(jnp.int32, (n_r, n_r), 0)
    rc = lax.broadcasted_iota(jnp.int32, (n_r, n_r), 1)
    lstrict = jnp.where(rc < rr, 1.0, 0.0).astype(BF16)

    def prefix_incl(mask_f):
        within = jnp.dot(mask_f.reshape(n_e * n_r, LANES).astype(BF16), tri,
                         preferred_element_type=F32).reshape(n_e, n_r, LANES)
        outs = []
        for e in range(n_e):
            off = jnp.dot(lstrict, within[e].astype(BF16), preferred_element_type=F32)
            outs.append(within[e] + off[:, LANES - 1:LANES])
        return outs

    gt = a > thr
    eq = a == thr
    need = capf - count(gt)
    eq_f = jnp.where(eq, 1.0, 0.0)
    eq_incl = prefix_incl(eq_f)
    j_row = lax.broadcasted_iota(jnp.int32, (1, cap), 1).astype(F32)
    r_col = lax.broadcasted_iota(jnp.int32, (n_r, 1), 0).astype(F32)
    sel_list = []
    for e in range(n_e):
        eq_excl = eq_incl[e] - eq_f[e]
        sel_list.append(jnp.where(gt[e] | (eq[e] & (eq_excl < need[e])), 1.0, 0.0))
    slot_incl = prefix_incl(jnp.stack(sel_list, axis=0))
    for e in range(n_e):
        s_e = slot_incl[e]
        c_e = s_e[:, LANES - 1:LANES]
        blk = jnp.sum(jnp.where(c_e <= j_row, 1.0, 0.0), axis=0, keepdims=True)
        onehot_t = jnp.where(r_col == blk, 1.0, 0.0).astype(BF16)
        s_hi = jnp.floor(s_e * (1.0 / 64.0))
        s_lo = s_e - 64.0 * s_hi
        rows_t = (64.0 * jnp.dot(s_hi.T.astype(BF16), onehot_t, preferred_element_type=F32)
                  + jnp.dot(s_lo.T.astype(BF16), onehot_t, preferred_element_type=F32))
        within = jnp.sum(jnp.where(rows_t <= j_row, 1.0, 0.0), axis=0, keepdims=True)
        idx_ref[e:e + 1, :] = (blk * float(LANES) + within).astype(jnp.int32)


def _route(aff3, cap):
    n_e, n_r, _ = aff3.shape
    return pl.pallas_call(
        functools.partial(_route_kernel, cap),
        grid=(1,),
        in_specs=[pl.BlockSpec((n_e, n_r, LANES), lambda i: (0, 0, 0))],
        out_specs=pl.BlockSpec((n_e, cap), lambda i: (0, 0)),
        out_shape=jax.ShapeDtypeStruct((n_e, cap), jnp.int32),
        compiler_params=_cparams(("arbitrary",)),
        name="route",
    )(aff3)


def _row_copy(src_hbm, dst_vmem, tok, r, sem):
    return pltpu.make_async_copy(src_hbm.at[pl.ds(tok, 1), :], dst_vmem.at[pl.ds(r, 1), :], sem)


def _row_copy_back(src_vmem, dst_hbm, tok, r, sem):
    return pltpu.make_async_copy(src_vmem.at[pl.ds(r, 1), :], dst_hbm.at[pl.ds(tok, 1), :], sem)


def _gather_start(idx_ref, src_hbm, buf, sem, n_rows):
    def body(r, c):
        _row_copy(src_hbm, buf, idx_ref[0, 0, r], r, sem).start()
        return c
    lax.fori_loop(0, n_rows, body, 0, unroll=8)


def _gather_wait(idx_ref, src_hbm, buf, sem, n_rows):
    def body(r, c):
        _row_copy(src_hbm, buf, idx_ref[0, 0, r], r, sem).wait()
        return c
    lax.fori_loop(0, n_rows, body, 0, unroll=8)


def _scatter_start(idx_ref, buf, dst_hbm, sem, n_rows):
    def body(r, c):
        _row_copy_back(buf, dst_hbm, idx_ref[0, 0, r], r, sem).start()
        return c
    lax.fori_loop(0, n_rows, body, 0, unroll=8)


def _scatter_wait(idx_ref, buf, dst_hbm, sem, n_rows):
    def body(r, c):
        _row_copy_back(buf, dst_hbm, idx_ref[0, 0, r], r, sem).wait()
        return c
    lax.fori_loop(0, n_rows, body, 0, unroll=8)


def _moe_gather_kernel(tiles_per_expert, d, idx_ref, idxn_ref, x1e_hbm, xe_ref, g_ref, bufs, sems):
    tm = bufs.shape[1]
    s = pl.program_id(0)
    last = pl.num_programs(0) - 1
    slot = s % 2
    cur = bufs.at[slot]
    nxt = bufs.at[1 - slot]

    @pl.when(s == 0)
    def _():
        _gather_start(idx_ref, x1e_hbm, cur, sems.at[slot], tm)

    @pl.when(s < last)
    def _():
        for r in range(tm):
            _row_copy(x1e_hbm, nxt, idxn_ref[0, 0, r], r, sems.at[1 - slot]).start()

    for r in range(tm):
        _row_copy(x1e_hbm, cur, idx_ref[0, 0, r], r, sems.at[slot]).wait()
    e = s // tiles_per_expert
    xe_ref[...] = cur[:, :d].astype(xe_ref.dtype)
    aff = cur[:, d:]
    lane = lax.broadcasted_iota(jnp.int32, aff.shape, 1)
    g = jnp.sum(jnp.where(lane == e, aff, 0.0), axis=-1, keepdims=True)
    g_ref[...] = jnp.broadcast_to(g, g_ref.shape)


def _moe_gather(idx3, x1e, cap, tm):
    n_tiles = idx3.shape[0]
    d = x1e.shape[1] - LANES
    rows = n_tiles * tm
    return pl.pallas_call(
        functools.partial(_moe_gather_kernel, cap // tm, d),
        grid=(n_tiles,),
        in_specs=[pl.BlockSpec((1, 1, tm), lambda i: (i, 0, 0), memory_space=pltpu.SMEM),
                  pl.BlockSpec((1, 1, tm), lambda i: (jnp.minimum(i + 1, n_tiles - 1), 0, 0),
                               memory_space=pltpu.SMEM),
                  pl.BlockSpec(memory_space=pl.ANY)],
        out_specs=[pl.BlockSpec((tm, d), lambda i: (i, 0)),
                   pl.BlockSpec((tm, LANES), lambda i: (i, 0))],
        out_shape=[jax.ShapeDtypeStruct((rows, d), BF16),
                   jax.ShapeDtypeStruct((rows, LANES), F32)],
        scratch_shapes=[pltpu.VMEM((2, tm, d + LANES), F32), pltpu.SemaphoreType.DMA((2,))],
        compiler_params=_cparams(("arbitrary",)),
        name="moe_gather",
    )(idx3, idx3, x1e)


def _moe_up_kernel(x_ref, wg_ref, wu_ref, h_ref):
    x = x_ref[...]
    a = jnp.dot(x, wg_ref[...].astype(BF16), preferred_element_type=F32)
    b = jnp.dot(x, wu_ref[...].astype(BF16), preferred_element_type=F32)
    h_ref[...] = (a * _sigmoid(a) * b).astype(h_ref.dtype)


def _moe_up(xe, w_gate, w_up, layer, cap, tm, tf):
    rows, d = xe.shape
    _, n_e, _, ff = w_gate.shape
    per_e = cap // tm
    return pl.pallas_call(
        _moe_up_kernel,
        grid=(n_e, ff // tf, per_e),
        in_specs=[pl.BlockSpec((tm, d), lambda e, f, i: (e * per_e + i, 0)),
                  pl.BlockSpec((None, None, d, tf), lambda e, f, i: (layer, e, 0, f)),
                  pl.BlockSpec((None, None, d, tf), lambda e, f, i: (layer, e, 0, f))],
        out_specs=pl.BlockSpec((tm, tf), lambda e, f, i: (e * per_e + i, f)),
        out_shape=jax.ShapeDtypeStruct((rows, ff), BF16),
        compiler_params=_cparams(("parallel", "parallel", "parallel")),
        name="moe_up",
    )(xe, w_gate, w_up)


def _moe_down_kernel(per_e, n_tok, idx_ref, idxp_ref, h_ref, wd_ref, g_ref, acc_in_hbm, acc_hbm,
                     bufs, gsem, ssem):
    del acc_in_hbm
    tm = bufs.shape[1]
    e = pl.program_id(0)
    s = e * per_e + pl.program_id(1)
    n_steps = pl.num_programs(0) * per_e
    slot = s % 2
    cur = bufs.at[slot]
    prv = bufs.at[1 - slot]
    off_cur = (e % 2) * n_tok
    s_prev = (s + n_steps - 1) % n_steps
    off_prv = ((s_prev // per_e) % 2) * n_tok

    @pl.when(s == 0)
    def _():
        def start(r, c):
            _row_copy(acc_hbm, prv, off_prv + idxp_ref[0, 0, r], r, ssem).start()
            return c
        lax.fori_loop(0, tm, start, 0, unroll=8)

        def wait(r, c):
            _row_copy(acc_hbm, prv, off_prv + idxp_ref[0, 0, r], r, ssem).wait()
            return c
        lax.fori_loop(0, tm, wait, 0, unroll=8)

    for r in range(tm):
        _row_copy(acc_hbm, cur, off_cur + idx_ref[0, 0, r], r, gsem).start()
    for r in range(tm):
        _row_copy_back(prv, acc_hbm, off_prv + idxp_ref[0, 0, r], r, ssem).start()
    ye = jnp.dot(h_ref[...], wd_ref[...], preferred_element_type=F32) * g_ref[:, 0:1]
    for r in range(tm):
        _row_copy(acc_hbm, cur, off_cur + idx_ref[0, 0, r], r, gsem).wait()
    for r in range(tm):
        _row_copy_back(prv, acc_hbm, off_prv + idxp_ref[0, 0, r], r, ssem).wait()
    cur[...] = cur[...] + ye

    @pl.when(s == n_steps - 1)
    def _():
        def start(r, c):
            _row_copy_back(cur, acc_hbm, off_cur + idx_ref[0, 0, r], r, ssem).start()
            return c
        lax.fori_loop(0, tm, start, 0, unroll=8)

        def wait(r, c):
            _row_copy_back(cur, acc_hbm, off_cur + idx_ref[0, 0, r], r, ssem).wait()
            return c
        lax.fori_loop(0, tm, wait, 0, unroll=8)


def _moe_down(idx3, h, w_down, g, acc, cap, tm):
    rows, ff = h.shape
    n_e, _, d = w_down.shape
    per_e = cap // tm
    n_steps = n_e * per_e
    n_tok = acc.shape[0] // 2
    blk = lambda e, i: e * per_e + i
    return pl.pallas_call(
        functools.partial(_moe_down_kernel, per_e, n_tok),
        grid=(n_e, per_e),
        in_specs=[pl.BlockSpec((1, 1, tm), lambda e, i: (blk(e, i), 0, 0), memory_space=pltpu.SMEM),
                  pl.BlockSpec((1, 1, tm), lambda e, i: ((blk(e, i) + n_steps - 1) % n_steps, 0, 0),
                               memory_space=pltpu.SMEM),
                  pl.BlockSpec((tm, ff), lambda e, i: (blk(e, i), 0)),
                  pl.BlockSpec((None, ff, d), lambda e, i: (e, 0, 0), pipeline_mode=pl.Buffered(1)),
                  pl.BlockSpec((tm, LANES), lambda e, i: (blk(e, i), 0)),
                  pl.BlockSpec(memory_space=pl.ANY)],
        out_specs=pl.BlockSpec(memory_space=pl.ANY),
        out_shape=jax.ShapeDtypeStruct(acc.shape, acc.dtype),
        scratch_shapes=[pltpu.VMEM((2, tm, d), F32), pltpu.SemaphoreType.DMA(()), pltpu.SemaphoreType.DMA(())],
        input_output_aliases={5: 0},
        compiler_params=_cparams(("arbitrary", "arbitrary")),
        name="moe_down",
    )(idx3, idx3, h, w_down, g, acc)


def _ln2_kernel(x_ref, g_ref, b_ref, y_ref, ybf_ref):
    y = _layer_norm_rows(x_ref[0] + x_ref[1], g_ref[...], b_ref[...])
    y_ref[...] = y
    ybf_ref[...] = y.astype(ybf_ref.dtype)


def _ln2(x, g, b, tm):
    _, m, d = x.shape
    row = lambda i: (i, 0)
    return pl.pallas_call(
        _ln2_kernel,
        grid=(m // tm,),
        in_specs=[pl.BlockSpec((2, tm, d), lambda i: (0, i, 0)),
                  pl.BlockSpec((1, d), lambda i: (0, 0)),
                  pl.BlockSpec((1, d), lambda i: (0, 0))],
        out_specs=[pl.BlockSpec((tm, d), row), pl.BlockSpec((tm, d), row)],
        out_shape=[jax.ShapeDtypeStruct((m, d), F32), jax.ShapeDtypeStruct((m, d), BF16)],
        compiler_params=_cparams(("parallel",)),
        name="ln2",
    )(x, g, b)


def _rope_tables(seq):
    rows = seq // GRID_W
    row = jnp.repeat(jnp.arange(rows), GRID_W).astype(F32)
    col = (jnp.arange(rows * GRID_W) % GRID_W).astype(F32)
    inv = ROPE_THETA ** (-jnp.arange(ROPE_FREQS, dtype=F32) / ROPE_FREQS)
    ang_r = row[:, None] * inv
    ang_c = col[:, None] * inv
    cos_t = jnp.concatenate([jnp.cos(ang_r), jnp.cos(ang_r), jnp.cos(ang_c), jnp.cos(ang_c)], axis=1)
    sin_t = jnp.concatenate([-jnp.sin(ang_r), jnp.sin(ang_r), -jnp.sin(ang_c), jnp.sin(ang_c)], axis=1)
    return cos_t, sin_t


def _prep_layer(l, w_in, b_gate, q_norm_g, k_norm_g, sgu_ln_g, sgu_ln_b, w_s, b_s, w_mem_kv, w_branch,
                w_o, ln1_g, ln1_b, w_router, w_gate, w_up, w_down, ln2_g, ln2_b):
    d = w_in.shape[1]
    wi = w_in[l]
    c_q, c_k, c_v = ATTN_WIDTH, ATTN_WIDTH + KV_WIDTH, ATTN_WIDTH + 2 * KV_WIDTH
    c_u = c_v + SGU_WIDTH
    c_vb = c_u + SGU_WIDTH
    c_qm = c_vb + MEM_WIDTH
    wr = jnp.pad(w_router[l], ((0, 0), (0, LANES - N_EXPERTS)))
    wr_hi = wr.astype(BF16)
    wr_lo = (wr - wr_hi.astype(F32)).astype(BF16)
    return dict(
        w_attn=jnp.concatenate([wi[:, :c_v], wi[:, c_vb:c_qm]], axis=1).astype(BF16),
        w_uv=wi[:, c_v:c_vb].astype(BF16),
        w_g=wi[:, c_qm:].astype(BF16),
        b_g=b_gate[l].reshape(1, N_BRANCH * d),
        gq=q_norm_g[l].reshape(1, HEAD_DIM), gk=k_norm_g[l].reshape(1, HEAD_DIM),
        sgu_g=sgu_ln_g[l].reshape(1, SGU_WIDTH), sgu_b=sgu_ln_b[l].reshape(1, SGU_WIDTH),
        w_s=w_s[l].astype(BF16), b_s=b_s[l].reshape(SGU_GROUPS, CHUNK, 1),
        w_mem_kv=w_mem_kv[l].astype(BF16), w_branch=w_branch[l].astype(BF16), w_o=w_o[l].astype(BF16),
        ln1_g=ln1_g[l].reshape(1, d), ln1_b=ln1_b[l].reshape(1, d),
        wr_hi=wr_hi, wr_lo=wr_lo,
        layer=l, w_gate=w_gate, w_up=w_up, w_down=w_down[l].astype(BF16),
        ln2_g=ln2_g[l].reshape(1, d), ln2_b=ln2_b[l].reshape(1, d),
    )


def _run_trunk(x, mem, layers, alpha):
    bsz, seq, d = x.shape
    n_mem = mem.shape[1]
    m = bsz * seq
    cap = EC_FACTOR * m // N_EXPERTS
    cos_t, sin_t = _rope_tables(seq)
    xf = x.reshape(m, d)
    x_bf = xf.astype(BF16)
    mem_bf = mem.reshape(bsz * n_mem, d).astype(BF16)
    tm_moe = min(512, cap)
    for p in layers:
        q, k, v, qm = _proj_attn(x_bf, p["w_attn"], cos_t, sin_t, p["gq"], p["gk"], seq, tm=512)
        out_b = _proj_sgu(x_bf, p["w_uv"], p["sgu_g"], p["sgu_b"], p["w_s"], p["b_s"], tm=512)
        out_a = _gqa(q, k, v, bsz, seq, tq=256)
        kv = _mm_plain(mem_bf, p["w_mem_kv"], tm=min(512, bsz * n_mem))
        out_c = _mem_attn(qm, kv, bsz, seq, n_mem, tq=512)
        merged = _branch_merge(x_bf, p["w_g"], p["b_g"], out_a, out_b, out_c, p["w_branch"], tm=512, tn=512)
        x1e, acc, aff_t = _out_ln1(merged, p["w_o"], xf, p["ln1_g"], p["ln1_b"], p["wr_hi"], p["wr_lo"],
                                   alpha, tm=256)
        idx = _route(aff_t.reshape(N_EXPERTS, m // LANES, LANES), cap)
        idx3 = idx.reshape(N_EXPERTS * cap // tm_moe, 1, tm_moe)
        xe, g = _moe_gather(idx3, x1e, cap, tm_moe)
        h = _moe_up(xe, p["w_gate"], p["w_up"], p["layer"], cap, tm=min(1024, cap),
                    tf=min(512, p["w_gate"].shape[3]))
        acc = _moe_down(idx3, h, p["w_down"], g, acc.reshape(2 * m, d), cap, tm_moe)
        xf, x_bf = _ln2(acc.reshape(2, m, d), p["ln2_g"], p["ln2_b"], tm=512)
    return xf.reshape(bsz, seq, d)


def kernel(x_prompt, x_sample, mem_prompt, mem_sample, w_in, b_gate, q_norm_g, k_norm_g, sgu_ln_g, sgu_ln_b,
           w_s, b_s, w_mem_kv, w_branch, w_o, ln1_g, ln1_b, w_router, w_gate, w_up, w_down, ln2_g, ln2_b):
    depth = w_in.shape[0]
    alpha = (2 * depth) ** 0.25
    layers = [_prep_layer(l, w_in, b_gate, q_norm_g, k_norm_g, sgu_ln_g, sgu_ln_b, w_s, b_s, w_mem_kv,
                          w_branch, w_o, ln1_g, ln1_b, w_router, w_gate, w_up, w_down, ln2_g, ln2_b)
              for l in range(depth)]
    y_prompt = _run_trunk(x_prompt, mem_prompt, layers, alpha)
    y_sample = _run_trunk(x_sample, mem_sample, layers, alpha)
    return (y_prompt, y_sample)
```

```python
import functools

import jax
import jax.numpy as jnp
from jax import lax
from jax.experimental import pallas as pl
from jax.experimental.pallas import tpu as pltpu

F32 = jnp.float32
BF16 = jnp.bfloat16

HEAD_DIM = 128
N_Q_HEADS = 8
N_KV_HEADS = 2
Q_PER_KV = N_Q_HEADS // N_KV_HEADS
ATTN_WIDTH = N_Q_HEADS * HEAD_DIM
KV_WIDTH = N_KV_HEADS * HEAD_DIM
SGU_GROUPS = 8
SGU_WIDTH = SGU_GROUPS * 128
CHUNK = 128
MEM_HEADS = 4
MEM_WIDTH = MEM_HEADS * HEAD_DIM
N_BRANCH = 3
N_EXPERTS = 16
EC_FACTOR = 2
GRID_W = 64
ROPE_THETA = 10000.0
ROPE_FREQS = HEAD_DIM // 4
EPS = 1e-6
LANES = 128
ROW_SUB = 256
VMEM_LIMIT = 56 * 1024 * 1024


def _cparams(sem, vmem=VMEM_LIMIT):
    return pltpu.CompilerParams(dimension_semantics=sem, vmem_limit_bytes=vmem)


def _gelu_tanh(x):
    return 0.5 * x * (1.0 + jnp.tanh(0.7978845608028654 * (x + 0.044715 * (x * x * x))))


def _sigmoid(x):
    return 1.0 / (1.0 + jnp.exp(-x))


def _layer_norm_rows(x, g, b):
    mu = jnp.mean(x, axis=-1, keepdims=True)
    xc = x - mu
    var = jnp.mean(xc * xc, axis=-1, keepdims=True)
    return xc * lax.rsqrt(var + EPS) * g + b


def _mm_plain_kernel(x_ref, w_ref, o_ref):
    o_ref[...] = jnp.dot(x_ref[...], w_ref[...], preferred_element_type=F32).astype(o_ref.dtype)


def _mm_plain(x, w, tm):
    m, k = x.shape
    n = w.shape[1]
    return pl.pallas_call(
        _mm_plain_kernel,
        grid=(m // tm,),
        in_specs=[pl.BlockSpec((tm, k), lambda i: (i, 0)),
                  pl.BlockSpec((k, n), lambda i: (0, 0))],
        out_specs=pl.BlockSpec((tm, n), lambda i: (i, 0)),
        out_shape=jax.ShapeDtypeStruct((m, n), BF16),
        compiler_params=_cparams(("parallel",)),
        name="mm_plain",
    )(x, w)


def _proj_attn_kernel(x_ref, w_ref, cos_ref, sin_ref, gq_ref, gk_ref, q_ref, k_ref, v_ref, qm_ref):
    scale = HEAD_DIM ** -0.5
    gq = gq_ref[...]
    gk = gk_ref[...]
    tm = x_ref.shape[0]
    for r0 in range(0, tm, ROW_SUB):
        rows = pl.ds(r0, ROW_SUB)
        acc = jnp.dot(x_ref[rows, :], w_ref[...], preferred_element_type=F32)
        cos = cos_ref[rows, :]
        sin = sin_ref[rows, :]
        lane = lax.broadcasted_iota(jnp.int32, cos.shape, 1)
        first_half = (lane % (2 * ROPE_FREQS)) < ROPE_FREQS

        def norm_rope(blk, g):
            ms = jnp.mean(blk * blk, axis=-1, keepdims=True)
            y = blk * lax.rsqrt(ms + EPS) * g
            partner = jnp.where(first_half,
                                pltpu.roll(y, HEAD_DIM - ROPE_FREQS, 1),
                                pltpu.roll(y, ROPE_FREQS, 1))
            return y * cos + partner * sin

        for h in range(N_Q_HEADS):
            blk = acc[:, h * HEAD_DIM:(h + 1) * HEAD_DIM]
            q_ref[rows, h * HEAD_DIM:(h + 1) * HEAD_DIM] = (norm_rope(blk, gq) * scale).astype(q_ref.dtype)
        for h in range(N_KV_HEADS):
            c0 = ATTN_WIDTH + h * HEAD_DIM
            k_ref[rows, h * HEAD_DIM:(h + 1) * HEAD_DIM] = norm_rope(acc[:, c0:c0 + HEAD_DIM], gk).astype(k_ref.dtype)
        c0 = ATTN_WIDTH + KV_WIDTH
        v_ref[rows, :] = acc[:, c0:c0 + KV_WIDTH].astype(v_ref.dtype)
        c0 += KV_WIDTH
        qm_ref[rows, :] = (acc[:, c0:c0 + MEM_WIDTH] * scale).astype(qm_ref.dtype)


def _proj_attn(x_bf, w_attn, cos_t, sin_t, gq, gk, seq, tm):
    m, d = x_bf.shape
    n = w_attn.shape[1]
    per_seq = seq // tm
    row = lambda i: (i, 0)
    const = lambda i: (0, 0)
    return pl.pallas_call(
        _proj_attn_kernel,
        grid=(m // tm,),
        in_specs=[pl.BlockSpec((tm, d), row),
                  pl.BlockSpec((d, n), const),
                  pl.BlockSpec((tm, HEAD_DIM), lambda i: (i % per_seq, 0)),
                  pl.BlockSpec((tm, HEAD_DIM), lambda i: (i % per_seq, 0)),
                  pl.BlockSpec((1, HEAD_DIM), const),
                  pl.BlockSpec((1, HEAD_DIM), const)],
        out_specs=[pl.BlockSpec((tm, ATTN_WIDTH), row),
                   pl.BlockSpec((tm, KV_WIDTH), row),
                   pl.BlockSpec((tm, KV_WIDTH), row),
                   pl.BlockSpec((tm, MEM_WIDTH), row)],
        out_shape=[jax.ShapeDtypeStruct((m, ATTN_WIDTH), BF16),
                   jax.ShapeDtypeStruct((m, KV_WIDTH), BF16),
                   jax.ShapeDtypeStruct((m, KV_WIDTH), BF16),
                   jax.ShapeDtypeStruct((m, MEM_WIDTH), BF16)],
        compiler_params=_cparams(("parallel",)),
        name="proj_attn",
    )(x_bf, w_attn, cos_t, sin_t, gq, gk)


def _proj_sgu_kernel(x_ref, w_ref, lng_ref, lnb_ref, ws_ref, bs_ref, o_ref):
    acc = jnp.dot(x_ref[...], w_ref[...], preferred_element_type=F32)
    u = _gelu_tanh(acc[:, :SGU_WIDTH])
    v = _gelu_tanh(acc[:, SGU_WIDTH:])
    vn = _layer_norm_rows(v, lng_ref[...], lnb_ref[...]).astype(BF16)
    tm = acc.shape[0]
    for c in range(tm // CHUNK):
        r0 = c * CHUNK
        for g in range(SGU_GROUPS):
            c0 = g * 128
            mixed = jnp.dot(ws_ref[g], vn[r0:r0 + CHUNK, c0:c0 + 128], preferred_element_type=F32)
            mixed = mixed + bs_ref[g]
            o_ref[r0:r0 + CHUNK, c0:c0 + 128] = (u[r0:r0 + CHUNK, c0:c0 + 128] * mixed).astype(o_ref.dtype)


def _proj_sgu(x_bf, w_uv, ln_g, ln_b, w_s, b_s, tm):
    m, d = x_bf.shape
    return pl.pallas_call(
        _proj_sgu_kernel,
        grid=(m // tm,),
        in_specs=[pl.BlockSpec((tm, d), lambda i: (i, 0)),
                  pl.BlockSpec((d, 2 * SGU_WIDTH), lambda i: (0, 0)),
                  pl.BlockSpec((1, SGU_WIDTH), lambda i: (0, 0)),
                  pl.BlockSpec((1, SGU_WIDTH), lambda i: (0, 0)),
                  pl.BlockSpec((SGU_GROUPS, CHUNK, CHUNK), lambda i: (0, 0, 0)),
                  pl.BlockSpec((SGU_GROUPS, CHUNK, 1), lambda i: (0, 0, 0))],
        out_specs=pl.BlockSpec((tm, SGU_WIDTH), lambda i: (i, 0)),
        out_shape=jax.ShapeDtypeStruct((m, SGU_WIDTH), BF16),
        compiler_params=_cparams(("parallel",)),
        name="proj_sgu",
    )(x_bf, w_uv, ln_g, ln_b, w_s, b_s)


def _softmax_pv(s, v):
    mx = jnp.max(s, axis=-1, keepdims=True)
    p = jnp.exp(s - mx)
    l = jnp.sum(p, axis=-1, keepdims=True)
    o = jnp.dot(p.astype(BF16), v, preferred_element_type=F32)
    return o / l


_NT = (((1,), (1,)), ((), ()))


def _gqa_kernel(q_ref, k_ref, v_ref, o_ref):
    k = k_ref[...]
    v = v_ref[...]
    for h in range(Q_PER_KV):
        q = q_ref[:, h * HEAD_DIM:(h + 1) * HEAD_DIM]
        s = lax.dot_general(q, k, _NT, preferred_element_type=F32)
        o_ref[:, h * HEAD_DIM:(h + 1) * HEAD_DIM] = _softmax_pv(s, v).astype(o_ref.dtype)


def _gqa(q, k, v, bsz, seq, tq):
    m = q.shape[0]
    nq = seq // tq
    gw = Q_PER_KV * HEAD_DIM
    return pl.pallas_call(
        _gqa_kernel,
        grid=(bsz, N_KV_HEADS, nq),
        in_specs=[pl.BlockSpec((tq, gw), lambda b, g, i: (b * nq + i, g)),
                  pl.BlockSpec((seq, HEAD_DIM), lambda b, g, i: (b, g)),
                  pl.BlockSpec((seq, HEAD_DIM), lambda b, g, i: (b, g))],
        out_specs=pl.BlockSpec((tq, gw), lambda b, g, i: (b * nq + i, g)),
        out_shape=jax.ShapeDtypeStruct((m, ATTN_WIDTH), BF16),
        compiler_params=_cparams(("parallel", "parallel", "parallel")),
        name="gqa",
    )(q, k, v)


def _mem_attn_kernel(q_ref, kv_ref, o_ref):
    for h in range(MEM_HEADS):
        q = q_ref[:, h * HEAD_DIM:(h + 1) * HEAD_DIM]
        k = kv_ref[:, h * HEAD_DIM:(h + 1) * HEAD_DIM]
        v = kv_ref[:, MEM_WIDTH + h * HEAD_DIM:MEM_WIDTH + (h + 1) * HEAD_DIM]
        s = lax.dot_general(q, k, _NT, preferred_element_type=F32)
        o_ref[:, h * HEAD_DIM:(h + 1) * HEAD_DIM] = _softmax_pv(s, v).astype(o_ref.dtype)


def _mem_attn(qm, kv, bsz, seq, n_mem, tq):
    m = qm.shape[0]
    nq = seq // tq
    return pl.pallas_call(
        _mem_attn_kernel,
        grid=(bsz, nq),
        in_specs=[pl.BlockSpec((tq, MEM_WIDTH), lambda b, i: (b * nq + i, 0)),
                  pl.BlockSpec((n_mem, 2 * MEM_WIDTH), lambda b, i: (b, 0))],
        out_specs=pl.BlockSpec((tq, MEM_WIDTH), lambda b, i: (b * nq + i, 0)),
        out_shape=jax.ShapeDtypeStruct((m, MEM_WIDTH), BF16),
        compiler_params=_cparams(("parallel", "parallel")),
        name="mem_attn",
    )(qm, kv)


def _branch_merge_kernel(x_ref, wg0, wg1, wg2, bg0, bg1, bg2, oa_ref, ob_ref, oc_ref,
                         wba, wbb, wbc, o_ref):
    x = x_ref[...]
    acc = None
    for wg, bg, o, wb in ((wg0, bg0, oa_ref, wba), (wg1, bg1, ob_ref, wbb), (wg2, bg2, oc_ref, wbc)):
        gate = _sigmoid(jnp.dot(x, wg[...], preferred_element_type=F32) + bg[...])
        p = jnp.dot(o[...], wb[...], preferred_element_type=F32)
        acc = gate * p if acc is None else acc + gate * p
    o_ref[...] = acc.astype(o_ref.dtype)


def _branch_merge(x_bf, w_g, b_g, oa, ob, oc, w_branch, tm, tn):
    m, d = x_bf.shape
    nj = d // tn
    row = lambda j, i: (i, 0)
    in_specs = [pl.BlockSpec((tm, d), row)]
    in_specs += [pl.BlockSpec((d, tn), functools.partial(lambda j, i, br: (0, br * nj + j), br=br))
                 for br in range(N_BRANCH)]
    in_specs += [pl.BlockSpec((1, tn), functools.partial(lambda j, i, br: (0, br * nj + j), br=br))
                 for br in range(N_BRANCH)]
    in_specs += [pl.BlockSpec((tm, ATTN_WIDTH), row),
                 pl.BlockSpec((tm, SGU_WIDTH), row),
                 pl.BlockSpec((tm, MEM_WIDTH), row),
                 pl.BlockSpec((ATTN_WIDTH, tn), lambda j, i: (0, j)),
                 pl.BlockSpec((SGU_WIDTH, tn), lambda j, i: (ATTN_WIDTH // SGU_WIDTH, j)),
                 pl.BlockSpec((MEM_WIDTH, tn), lambda j, i: ((ATTN_WIDTH + SGU_WIDTH) // MEM_WIDTH, j))]
    return pl.pallas_call(
        _branch_merge_kernel,
        grid=(nj, m // tm),
        in_specs=in_specs,
        out_specs=pl.BlockSpec((tm, tn), lambda j, i: (i, j)),
        out_shape=jax.ShapeDtypeStruct((m, d), BF16),
        compiler_params=_cparams(("parallel", "parallel")),
        name="branch_merge",
    )(x_bf, w_g, w_g, w_g, b_g, b_g, b_g, oa, ob, oc, w_branch, w_branch, w_branch)


def _split_bf16(x):
    hi = x.astype(BF16)
    lo = (x - hi.astype(F32)).astype(BF16)
    return hi, lo


def _out_ln1_kernel(alpha, mg_ref, wo_ref, x_ref, g_ref, b_ref, wr_ref,
                    x1e_ref, xs_ref, afft_ref):
    tm, d = x_ref.shape
    sub = min(ROW_SUB, tm)
    for r0 in range(0, tm, sub):
        rows = pl.ds(r0, sub)
        y = jnp.dot(mg_ref[rows, :], wo_ref[...], preferred_element_type=F32)
        x1 = _layer_norm_rows(alpha * x_ref[rows, :] + y, g_ref[...], b_ref[...])
        x1e_ref[rows, :d] = x1
        xs_ref[0, rows, :] = alpha * x1
        xs_ref[1, rows, :] = jnp.zeros_like(x1)
        xh, xl = _split_bf16(x1)
        wrh = wr_ref[:, :LANES]
        logits = (jnp.dot(xh, wrh, preferred_element_type=F32)
                  + jnp.dot(xh, wr_ref[:, LANES:], preferred_element_type=F32)
                  + jnp.dot(xl, wrh, preferred_element_type=F32))
        lane = lax.broadcasted_iota(jnp.int32, logits.shape, 1)
        valid = lane < N_EXPERTS
        logits = jnp.where(valid, logits, -1e30)
        mx = jnp.max(logits, axis=-1, keepdims=True)
        e = jnp.where(valid, jnp.exp(logits - mx), 0.0)
        aff = e / jnp.sum(e, axis=-1, keepdims=True)
        x1e_ref[rows, d:] = aff
        afft_ref[:, rows] = aff.T[:N_EXPERTS, :]


def _out_ln1(merged, w_o, x, ln_g, ln_b, wr, alpha, tm):
    m, d = x.shape
    row = lambda i: (i, 0)
    const = lambda i: (0, 0)
    return pl.pallas_call(
        functools.partial(_out_ln1_kernel, alpha),
        grid=(m // tm,),
        in_specs=[pl.BlockSpec((tm, d), row),
                  pl.BlockSpec((d, d), const, pipeline_mode=pl.Buffered(1)),
                  pl.BlockSpec((tm, d), row),
                  pl.BlockSpec((1, d), const),
                  pl.BlockSpec((1, d), const),
                  pl.BlockSpec((d, 2 * LANES), const, pipeline_mode=pl.Buffered(1))],
        out_specs=[pl.BlockSpec((tm, d + LANES), row),
                   pl.BlockSpec((2, tm, d), lambda i: (0, i, 0)),
                   pl.BlockSpec((N_EXPERTS, tm), lambda i: (0, i))],
        out_shape=[jax.ShapeDtypeStruct((m, d + LANES), F32),
                   jax.ShapeDtypeStruct((2, m, d), F32),
                   jax.ShapeDtypeStruct((N_EXPERTS, m), F32)],
        compiler_params=_cparams(("parallel",)),
        name="out_ln1",
    )(merged, w_o, x, ln_g, ln_b, wr)


_BISECT_STEPS = 40


def _route_kernel(cap, a_ref, idx_ref):
    a = a_ref[...]
    n_e, n_r, _ = a.shape
    capf = float(cap)

    def count(mask):
        part = jnp.sum(jnp.where(mask, 1.0, 0.0), axis=2, keepdims=True)
        return jnp.sum(part, axis=1, keepdims=True)

    def bisect(_, carry):
        lo, hi = carry
        mid = 0.5 * (lo + hi)
        ok = count(a >= mid) >= capf
        return jnp.where(ok, mid, lo), jnp.where(ok, hi, mid)

    lo0 = jnp.zeros((n_e, 1, 1), F32)
    hi0 = jnp.full((n_e, 1, 1), 2.0, F32)
    _, hi = lax.fori_loop(0, _BISECT_STEPS, bisect, (lo0, hi0))

    def below_max(hi):
        part = jnp.max(jnp.where(a < hi, a, -1.0), axis=2, keepdims=True)
        return jnp.max(part, axis=1, keepdims=True)

    def refine_cond(carry):
        hi, t = carry
        short = jnp.where(count(a >= t) < capf, 1.0, 0.0)
        return jnp.max(short) > 0.0

    def refine_body(carry):
        hi, t = carry
        hi = jnp.where(count(a >= t) < capf, t, hi)
        return hi, below_max(hi)

    _, thr = lax.while_loop(refine_cond, refine_body, (hi, below_max(hi)))

    ri = lax.broadcasted_iota(jnp.int32, (LANES, LANES), 0)
    ci = lax.broadcasted_iota(jnp.int32, (LANES, LANES), 1)
    tri = jnp.where(ri <= ci, 1.0, 0.0).astype(BF16)
    rr = lax.broadcasted_iota(jnp.int32, (n_r, n_r), 0)
    rc = lax.broadcasted_iota(jnp.int32, (n_r, n_r), 1)
    lstrict = jnp.where(rc < rr, 1.0, 0.0).astype(BF16)

    def prefix_incl(mask_f):
        within = jnp.dot(mask_f.reshape(n_e * n_r, LANES).astype(BF16), tri,
                         preferred_element_type=F32).reshape(n_e, n_r, LANES)
        outs = []
        for e in range(n_e):
            off = jnp.dot(lstrict, within[e].astype(BF16), preferred_element_type=F32)
            outs.append(within[e] + off[:, LANES - 1:LANES])
        return outs

    gt = a > thr
    eq = a == thr
    need = capf - count(gt)
    eq_f = jnp.where(eq, 1.0, 0.0)
    eq_incl = prefix_incl(eq_f)
    j_row = lax.broadcasted_iota(jnp.int32, (1, cap), 1).astype(F32)
    r_col = lax.broadcasted_iota(jnp.int32, (n_r, 1), 0).astype(F32)
    sel_list = []
    for e in range(n_e):
        eq_excl = eq_incl[e] - eq_f[e]
        sel_list.append(jnp.where(gt[e] | (eq[e] & (eq_excl < need[e])), 1.0, 0.0))
    slot_incl = prefix_incl(jnp.stack(sel_list, axis=0))
    for e in range(n_e):
        s_e = slot_incl[e]
        c_e = s_e[:, LANES - 1:LANES]
        blk = jnp.sum(jnp.where(c_e <= j_row, 1.0, 0.0), axis=0, keepdims=True)
        onehot_t = jnp.where(r_col == blk, 1.0, 0.0).astype(BF16)
        s_hi = jnp.floor(s_e * (1.0 / 64.0))
        s_lo = s_e - 64.0 * s_hi
        rows_t = (64.0 * jnp.dot(s_hi.T.astype(BF16), onehot_t, preferred_element_type=F32)
                  + jnp.dot(s_lo.T.astype(BF16), onehot_t, preferred_element_type=F32))
        within = jnp.sum(jnp.where(rows_t <= j_row, 1.0, 0.0), axis=0, keepdims=True)
        idx_ref[e:e + 1, :] = (blk * float(LANES) + within).astype(jnp.int32)


def _route(aff3, cap):
    n_e, n_r, _ = aff3.shape
    return pl.pallas_call(
        functools.partial(_route_kernel, cap),
        grid=(1,),
        in_specs=[pl.BlockSpec((n_e, n_r, LANES), lambda i: (0, 0, 0))],
        out_specs=pl.BlockSpec((n_e, cap), lambda i: (0, 0)),
        out_shape=jax.ShapeDtypeStruct((n_e, cap), jnp.int32),
        compiler_params=_cparams(("arbitrary",)),
        name="route",
    )(aff3)


def _row_copy(src_hbm, dst_vmem, tok, r, sem):
    return pltpu.make_async_copy(src_hbm.at[pl.ds(tok, 1), :], dst_vmem.at[pl.ds(r, 1), :], sem)


def _row_copy_back(src_vmem, dst_hbm, tok, r, sem):
    return pltpu.make_async_copy(src_vmem.at[pl.ds(r, 1), :], dst_hbm.at[pl.ds(tok, 1), :], sem)


def _gather_start(idx_ref, src_hbm, buf, sem, n_rows):
    def body(r, c):
        _row_copy(src_hbm, buf, idx_ref[0, 0, r], r, sem).start()
        return c
    lax.fori_loop(0, n_rows, body, 0, unroll=8)


def _gather_wait(idx_ref, src_hbm, buf, sem, n_rows):
    def body(r, c):
        _row_copy(src_hbm, buf, idx_ref[0, 0, r], r, sem).wait()
        return c
    lax.fori_loop(0, n_rows, body, 0, unroll=8)


def _scatter_start(idx_ref, buf, dst_hbm, sem, n_rows):
    def body(r, c):
        _row_copy_back(buf, dst_hbm, idx_ref[0, 0, r], r, sem).start()
        return c
    lax.fori_loop(0, n_rows, body, 0, unroll=8)


def _scatter_wait(idx_ref, buf, dst_hbm, sem, n_rows):
    def body(r, c):
        _row_copy_back(buf, dst_hbm, idx_ref[0, 0, r], r, sem).wait()
        return c
    lax.fori_loop(0, n_rows, body, 0, unroll=8)


def _moe_gather_kernel(tiles_per_expert, d, idx_ref, idxn_ref, x1e_hbm, xe_ref, g_ref, bufs, sems):
    tm = bufs.shape[1]
    s = pl.program_id(0)
    last = pl.num_programs(0) - 1
    slot = s % 2
    cur = bufs.at[slot]
    nxt = bufs.at[1 - slot]

    @pl.when(s == 0)
    def _():
        _gather_start(idx_ref, x1e_hbm, cur, sems.at[slot], tm)

    @pl.when(s < last)
    def _():
        for r in range(tm):
            _row_copy(x1e_hbm, nxt, idxn_ref[0, 0, r], r, sems.at[1 - slot]).start()

    for r in range(tm):
        _row_copy(x1e_hbm, cur, idx_ref[0, 0, r], r, sems.at[slot]).wait()
    e = s // tiles_per_expert
    xe_ref[...] = cur[:, :d].astype(xe_ref.dtype)
    aff = cur[:, d:]
    lane = lax.broadcasted_iota(jnp.int32, aff.shape, 1)
    g = jnp.sum(jnp.where(lane == e, aff, 0.0), axis=-1, keepdims=True)
    g_ref[...] = jnp.broadcast_to(g, g_ref.shape)


def _moe_gather(idx3, x1e, cap, tm):
    n_tiles = idx3.shape[0]
    d = x1e.shape[1] - LANES
    rows = n_tiles * tm
    return pl.pallas_call(
        functools.partial(_moe_gather_kernel, cap // tm, d),
        grid=(n_tiles,),
        in_specs=[pl.BlockSpec((1, 1, tm), lambda i: (i, 0, 0), memory_space=pltpu.SMEM),
                  pl.BlockSpec((1, 1, tm), lambda i: (jnp.minimum(i + 1, n_tiles - 1), 0, 0),
                               memory_space=pltpu.SMEM),
                  pl.BlockSpec(memory_space=pl.ANY)],
        out_specs=[pl.BlockSpec((tm, d), lambda i: (i, 0)),
                   pl.BlockSpec((tm, LANES), lambda i: (i, 0))],
        out_shape=[jax.ShapeDtypeStruct((rows, d), BF16),
                   jax.ShapeDtypeStruct((rows, LANES), F32)],
        scratch_shapes=[pltpu.VMEM((2, tm, d + LANES), F32), pltpu.SemaphoreType.DMA((2,))],
        compiler_params=_cparams(("arbitrary",)),
        name="moe_gather",
    )(idx3, idx3, x1e)


def _moe_up_kernel(cast_wd, x_ref, wg_ref, wu_ref, *rest):
    x = x_ref[...]
    a = jnp.dot(x, wg_ref[...].astype(BF16), preferred_element_type=F32)
    b = jnp.dot(x, wu_ref[...].astype(BF16), preferred_element_type=F32)
    if cast_wd:
        wd_ref, h_ref, wd_out_ref = rest
        @pl.when(pl.program_id(2) == 0)
        def _():
            wd_out_ref[...] = wd_ref[...].astype(wd_out_ref.dtype)
    else:
        h_ref, = rest
    h_ref[...] = (a * _sigmoid(a) * b).astype(h_ref.dtype)


def _moe_up(xe, w_gate, w_up, w_down, layer, cap, tm, tf):
    rows, d = xe.shape
    _, n_e, _, ff = w_gate.shape
    per_e = cap // tm
    cast_wd = w_down is not None
    in_specs = [pl.BlockSpec((tm, d), lambda e, f, i: (e * per_e + i, 0)),
                pl.BlockSpec((None, None, d, tf), lambda e, f, i: (layer, e, 0, f)),
                pl.BlockSpec((None, None, d, tf), lambda e, f, i: (layer, e, 0, f))]
    out_specs = [pl.BlockSpec((tm, tf), lambda e, f, i: (e * per_e + i, f))]
    out_shape = [jax.ShapeDtypeStruct((rows, ff), BF16)]
    args = [xe, w_gate, w_up]
    if cast_wd:
        in_specs.append(pl.BlockSpec((None, None, tf, d), lambda e, f, i: (layer, e, f, 0)))
        out_specs.append(pl.BlockSpec((None, tf, d), lambda e, f, i: (e, f, 0)))
        out_shape.append(jax.ShapeDtypeStruct((n_e, ff, d), BF16))
        args.append(w_down)
    res = pl.pallas_call(
        functools.partial(_moe_up_kernel, cast_wd),
        grid=(n_e, ff // tf, per_e),
        in_specs=in_specs,
        out_specs=out_specs,
        out_shape=out_shape,
        compiler_params=_cparams(("parallel", "parallel", "arbitrary")),
        name="moe_up_cast" if cast_wd else "moe_up",
    )(*args)
    return res if cast_wd else res[0]


def _moe_down_kernel(per_e, n_tok, idx_ref, idxp_ref, h_ref, wd_ref, g_ref, acc_in_hbm, acc_hbm,
                     bufs, gsem, ssem):
    del acc_in_hbm
    tm = bufs.shape[1]
    e = pl.program_id(0)
    s = e * per_e + pl.program_id(1)
    n_steps = pl.num_programs(0) * per_e
    slot = s % 2
    cur = bufs.at[slot]
    prv = bufs.at[1 - slot]
    off_cur = (e % 2) * n_tok
    s_prev = (s + n_steps - 1) % n_steps
    off_prv = ((s_prev // per_e) % 2) * n_tok

    @pl.when(s == 0)
    def _():
        def start(r, c):
            _row_copy(acc_hbm, prv, off_prv + idxp_ref[0, 0, r], r, ssem).start()
            return c
        lax.fori_loop(0, tm, start, 0, unroll=8)

        def wait(r, c):
            _row_copy(acc_hbm, prv, off_prv + idxp_ref[0, 0, r], r, ssem).wait()
            return c
        lax.fori_loop(0, tm, wait, 0, unroll=8)

    for r in range(tm):
        _row_copy(acc_hbm, cur, off_cur + idx_ref[0, 0, r], r, gsem).start()
    for r in range(tm):
        _row_copy_back(prv, acc_hbm, off_prv + idxp_ref[0, 0, r], r, ssem).start()
    ye = jnp.dot(h_ref[...], wd_ref[...], preferred_element_type=F32) * g_ref[:, 0:1]
    for r in range(tm):
        _row_copy(acc_hbm, cur, off_cur + idx_ref[0, 0, r], r, gsem).wait()
    for r in range(tm):
        _row_copy_back(prv, acc_hbm, off_prv + idxp_ref[0, 0, r], r, ssem).wait()
    cur[...] = cur[...] + ye

    @pl.when(s == n_steps - 1)
    def _():
        def start(r, c):
            _row_copy_back(cur, acc_hbm, off_cur + idx_ref[0, 0, r], r, ssem).start()
            return c
        lax.fori_loop(0, tm, start, 0, unroll=8)

        def wait(r, c):
            _row_copy_back(cur, acc_hbm, off_cur + idx_ref[0, 0, r], r, ssem).wait()
            return c
        lax.fori_loop(0, tm, wait, 0, unroll=8)


def _moe_down(idx3, h, w_down, g, acc, cap, tm):
    rows, ff = h.shape
    n_e, _, d = w_down.shape
    per_e = cap // tm
    n_steps = n_e * per_e
    n_tok = acc.shape[0] // 2
    blk = lambda e, i: e * per_e + i
    return pl.pallas_call(
        functools.partial(_moe_down_kernel, per_e, n_tok),
        grid=(n_e, per_e),
        in_specs=[pl.BlockSpec((1, 1, tm), lambda e, i: (blk(e, i), 0, 0), memory_space=pltpu.SMEM),
                  pl.BlockSpec((1, 1, tm), lambda e, i: ((blk(e, i) + n_steps - 1) % n_steps, 0, 0),
                               memory_space=pltpu.SMEM),
                  pl.BlockSpec((tm, ff), lambda e, i: (blk(e, i), 0)),
                  pl.BlockSpec((None, ff, d), lambda e, i: (e, 0, 0), pipeline_mode=pl.Buffered(1)),
                  pl.BlockSpec((tm, LANES), lambda e, i: (blk(e, i), 0)),
                  pl.BlockSpec(memory_space=pl.ANY)],
        out_specs=pl.BlockSpec(memory_space=pl.ANY),
        out_shape=jax.ShapeDtypeStruct(acc.shape, acc.dtype),
        scratch_shapes=[pltpu.VMEM((2, tm, d), F32), pltpu.SemaphoreType.DMA(()), pltpu.SemaphoreType.DMA(())],
        input_output_aliases={5: 0},
        compiler_params=_cparams(("arbitrary", "arbitrary")),
        name="moe_down",
    )(idx3, idx3, h, w_down, g, acc)


def _ln2_kernel(x_ref, g_ref, b_ref, y_ref, ybf_ref):
    y = _layer_norm_rows(x_ref[0] + x_ref[1], g_ref[...], b_ref[...])
    y_ref[...] = y
    ybf_ref[...] = y.astype(ybf_ref.dtype)


def _ln2(x, g, b, tm):
    _, m, d = x.shape
    row = lambda i: (i, 0)
    return pl.pallas_call(
        _ln2_kernel,
        grid=(m // tm,),
        in_specs=[pl.BlockSpec((2, tm, d), lambda i: (0, i, 0)),
                  pl.BlockSpec((1, d), lambda i: (0, 0)),
                  pl.BlockSpec((1, d), lambda i: (0, 0))],
        out_specs=[pl.BlockSpec((tm, d), row), pl.BlockSpec((tm, d), row)],
        out_shape=[jax.ShapeDtypeStruct((m, d), F32), jax.ShapeDtypeStruct((m, d), BF16)],
        compiler_params=_cparams(("parallel",)),
        name="ln2",
    )(x, g, b)


def _rope_tables(seq):
    rows = seq // GRID_W
    row = jnp.repeat(jnp.arange(rows), GRID_W).astype(F32)
    col = (jnp.arange(rows * GRID_W) % GRID_W).astype(F32)
    inv = ROPE_THETA ** (-jnp.arange(ROPE_FREQS, dtype=F32) / ROPE_FREQS)
    ang_r = row[:, None] * inv
    ang_c = col[:, None] * inv
    cos_t = jnp.concatenate([jnp.cos(ang_r), jnp.cos(ang_r), jnp.cos(ang_c), jnp.cos(ang_c)], axis=1)
    sin_t = jnp.concatenate([-jnp.sin(ang_r), jnp.sin(ang_r), -jnp.sin(ang_c), jnp.sin(ang_c)], axis=1)
    return cos_t, sin_t


def _prep_layer(l, w_in, b_gate, q_norm_g, k_norm_g, sgu_ln_g, sgu_ln_b, w_s, b_s, w_mem_kv, w_branch,
                w_o, ln1_g, ln1_b, w_router, w_gate, w_up, w_down, ln2_g, ln2_b):
    d = w_in.shape[1]
    wi = w_in[l]
    c_q, c_k, c_v = ATTN_WIDTH, ATTN_WIDTH + KV_WIDTH, ATTN_WIDTH + 2 * KV_WIDTH
    c_u = c_v + SGU_WIDTH
    c_vb = c_u + SGU_WIDTH
    c_qm = c_vb + MEM_WIDTH
    wr = jnp.pad(w_router[l], ((0, 0), (0, LANES - N_EXPERTS)))
    wr_hi = wr.astype(BF16)
    wr_lo = (wr - wr_hi.astype(F32)).astype(BF16)
    return dict(
        w_attn=jnp.concatenate([wi[:, :c_v], wi[:, c_vb:c_qm]], axis=1).astype(BF16),
        w_uv=wi[:, c_v:c_vb].astype(BF16),
        w_g=wi[:, c_qm:].astype(BF16),
        b_g=b_gate[l].reshape(1, N_BRANCH * d),
        gq=q_norm_g[l].reshape(1, HEAD_DIM), gk=k_norm_g[l].reshape(1, HEAD_DIM),
        sgu_g=sgu_ln_g[l].reshape(1, SGU_WIDTH), sgu_b=sgu_ln_b[l].reshape(1, SGU_WIDTH),
        w_s=w_s[l].astype(BF16), b_s=b_s[l].reshape(SGU_GROUPS, CHUNK, 1),
        w_mem_kv=w_mem_kv[l].astype(BF16), w_branch=w_branch[l].astype(BF16), w_o=w_o[l].astype(BF16),
        ln1_g=ln1_g[l].reshape(1, d), ln1_b=ln1_b[l].reshape(1, d),
        wr=jnp.concatenate([wr_hi, wr_lo], axis=1),
        layer=l, w_gate=w_gate, w_up=w_up, w_down=w_down,
        ln2_g=ln2_g[l].reshape(1, d), ln2_b=ln2_b[l].reshape(1, d),
    )


def _run_trunk(x, mem, layers, alpha, wd_bf16):
    bsz, seq, d = x.shape
    n_mem = mem.shape[1]
    m = bsz * seq
    cap = EC_FACTOR * m // N_EXPERTS
    cos_t, sin_t = _rope_tables(seq)
    xf = x.reshape(m, d)
    x_bf = xf.astype(BF16)
    mem_bf = mem.reshape(bsz * n_mem, d).astype(BF16)
    tm_moe = min(512, cap)
    for p in layers:
        q, k, v, qm = _proj_attn(x_bf, p["w_attn"], cos_t, sin_t, p["gq"], p["gk"], seq, tm=512)
        out_b = _proj_sgu(x_bf, p["w_uv"], p["sgu_g"], p["sgu_b"], p["w_s"], p["b_s"], tm=512)
        out_a = _gqa(q, k, v, bsz, seq, tq=256)
        kv = _mm_plain(mem_bf, p["w_mem_kv"], tm=min(512, bsz * n_mem))
        out_c = _mem_attn(qm, kv, bsz, seq, n_mem, tq=512)
        merged = _branch_merge(x_bf, p["w_g"], p["b_g"], out_a, out_b, out_c, p["w_branch"], tm=512, tn=512)
        x1e, acc, aff_t = _out_ln1(merged, p["w_o"], xf, p["ln1_g"], p["ln1_b"], p["wr"], alpha, tm=512)
        idx = _route(aff_t.reshape(N_EXPERTS, m // LANES, LANES), cap)
        idx3 = idx.reshape(N_EXPERTS * cap // tm_moe, 1, tm_moe)
        xe, g = _moe_gather(idx3, x1e, cap, tm_moe)
        cast_wd = p["layer"] not in wd_bf16
        res = _moe_up(xe, p["w_gate"], p["w_up"], p["w_down"] if cast_wd else None, p["layer"], cap,
                      tm=min(1024, cap), tf=min(512, p["w_gate"].shape[3]))
        if cast_wd:
            h, wd_bf16[p["layer"]] = res
        else:
            h = res
        acc = _moe_down(idx3, h, wd_bf16[p["layer"]], g, acc.reshape(2 * m, d), cap, tm_moe)
        xf, x_bf = _ln2(acc.reshape(2, m, d), p["ln2_g"], p["ln2_b"], tm=512)
    return xf.reshape(bsz, seq, d)


def kernel(x_prompt, x_sample, mem_prompt, mem_sample, w_in, b_gate, q_norm_g, k_norm_g, sgu_ln_g, sgu_ln_b,
           w_s, b_s, w_mem_kv, w_branch, w_o, ln1_g, ln1_b, w_router, w_gate, w_up, w_down, ln2_g, ln2_b):
    depth = w_in.shape[0]
    alpha = (2 * depth) ** 0.25
    layers = [_prep_layer(l, w_in, b_gate, q_norm_g, k_norm_g, sgu_ln_g, sgu_ln_b, w_s, b_s, w_mem_kv,
                          w_branch, w_o, ln1_g, ln1_b, w_router, w_gate, w_up, w_down, ln2_g, ln2_b)
              for l in range(depth)]
    wd_bf16 = {}
    y_prompt = _run_trunk(x_prompt, mem_prompt, layers, alpha, wd_bf16)
    y_sample = _run_trunk(x_sample, mem_sample, layers, alpha, wd_bf16)
    return (y_prompt, y_sample)
```

```python
import functools

import jax
import jax.numpy as jnp
from jax import lax
from jax.experimental import pallas as pl
from jax.experimental.pallas import tpu as pltpu

F32 = jnp.float32
BF16 = jnp.bfloat16

HEAD_DIM = 128
N_Q_HEADS = 8
N_KV_HEADS = 2
Q_PER_KV = N_Q_HEADS // N_KV_HEADS
ATTN_WIDTH = N_Q_HEADS * HEAD_DIM
KV_WIDTH = N_KV_HEADS * HEAD_DIM
SGU_GROUPS = 8
SGU_WIDTH = SGU_GROUPS * 128
CHUNK = 128
MEM_HEADS = 4
MEM_WIDTH = MEM_HEADS * HEAD_DIM
N_BRANCH = 3
N_EXPERTS = 16
EC_FACTOR = 2
GRID_W = 64
ROPE_THETA = 10000.0
ROPE_FREQS = HEAD_DIM // 4
EPS = 1e-6
LANES = 128
ROW_SUB = 256
VMEM_LIMIT = 56 * 1024 * 1024


def _cparams(sem, vmem=VMEM_LIMIT):
    return pltpu.CompilerParams(dimension_semantics=sem, vmem_limit_bytes=vmem)


def _gelu_tanh(x):
    return 0.5 * x * (1.0 + jnp.tanh(0.7978845608028654 * (x + 0.044715 * (x * x * x))))


def _sigmoid(x):
    return 1.0 / (1.0 + jnp.exp(-x))


def _layer_norm_rows(x, g, b):
    mu = jnp.mean(x, axis=-1, keepdims=True)
    xc = x - mu
    var = jnp.mean(xc * xc, axis=-1, keepdims=True)
    return xc * lax.rsqrt(var + EPS) * g + b


def _mm_plain_kernel(x_ref, w_ref, o_ref):
    o_ref[...] = jnp.dot(x_ref[...], w_ref[...], preferred_element_type=F32).astype(o_ref.dtype)


def _mm_plain(x, w, tm):
    m, k = x.shape
    n = w.shape[1]
    return pl.pallas_call(
        _mm_plain_kernel,
        grid=(m // tm,),
        in_specs=[pl.BlockSpec((tm, k), lambda i: (i, 0)),
                  pl.BlockSpec((k, n), lambda i: (0, 0))],
        out_specs=pl.BlockSpec((tm, n), lambda i: (i, 0)),
        out_shape=jax.ShapeDtypeStruct((m, n), BF16),
        compiler_params=_cparams(("parallel",)),
        name="mm_plain",
    )(x, w)


def _proj_attn_kernel(x_ref, w_ref, cos_ref, sin_ref, gq_ref, gk_ref, q_ref, k_ref, v_ref, qm_ref):
    scale = HEAD_DIM ** -0.5
    gq = gq_ref[...]
    gk = gk_ref[...]
    tm = x_ref.shape[0]
    for r0 in range(0, tm, ROW_SUB):
        rows = pl.ds(r0, ROW_SUB)
        acc = jnp.dot(x_ref[rows, :], w_ref[...], preferred_element_type=F32)
        cos = cos_ref[rows, :]
        sin = sin_ref[rows, :]
        lane = lax.broadcasted_iota(jnp.int32, cos.shape, 1)
        first_half = (lane % (2 * ROPE_FREQS)) < ROPE_FREQS

        def norm_rope(blk, g):
            ms = jnp.mean(blk * blk, axis=-1, keepdims=True)
            y = blk * lax.rsqrt(ms + EPS) * g
            partner = jnp.where(first_half,
                                pltpu.roll(y, HEAD_DIM - ROPE_FREQS, 1),
                                pltpu.roll(y, ROPE_FREQS, 1))
            return y * cos + partner * sin

        for h in range(N_Q_HEADS):
            blk = acc[:, h * HEAD_DIM:(h + 1) * HEAD_DIM]
            q_ref[rows, h * HEAD_DIM:(h + 1) * HEAD_DIM] = (norm_rope(blk, gq) * scale).astype(q_ref.dtype)
        for h in range(N_KV_HEADS):
            c0 = ATTN_WIDTH + h * HEAD_DIM
            k_ref[rows, h * HEAD_DIM:(h + 1) * HEAD_DIM] = norm_rope(acc[:, c0:c0 + HEAD_DIM], gk).astype(k_ref.dtype)
        c0 = ATTN_WIDTH + KV_WIDTH
        v_ref[rows, :] = acc[:, c0:c0 + KV_WIDTH].astype(v_ref.dtype)
        c0 += KV_WIDTH
        qm_ref[rows, :] = (acc[:, c0:c0 + MEM_WIDTH] * scale).astype(qm_ref.dtype)


def _proj_attn(x_bf, w_attn, cos_t, sin_t, gq, gk, seq, tm):
    m, d = x_bf.shape
    n = w_attn.shape[1]
    per_seq = seq // tm
    row = lambda i: (i, 0)
    const = lambda i: (0, 0)
    return pl.pallas_call(
        _proj_attn_kernel,
        grid=(m // tm,),
        in_specs=[pl.BlockSpec((tm, d), row),
                  pl.BlockSpec((d, n), const),
                  pl.BlockSpec((tm, HEAD_DIM), lambda i: (i % per_seq, 0)),
                  pl.BlockSpec((tm, HEAD_DIM), lambda i: (i % per_seq, 0)),
                  pl.BlockSpec((1, HEAD_DIM), const),
                  pl.BlockSpec((1, HEAD_DIM), const)],
        out_specs=[pl.BlockSpec((tm, ATTN_WIDTH), row),
                   pl.BlockSpec((tm, KV_WIDTH), row),
                   pl.BlockSpec((tm, KV_WIDTH), row),
                   pl.BlockSpec((tm, MEM_WIDTH), row)],
        out_shape=[jax.ShapeDtypeStruct((m, ATTN_WIDTH), BF16),
                   jax.ShapeDtypeStruct((m, KV_WIDTH), BF16),
                   jax.ShapeDtypeStruct((m, KV_WIDTH), BF16),
                   jax.ShapeDtypeStruct((m, MEM_WIDTH), BF16)],
        compiler_params=_cparams(("parallel",)),
        name="proj_attn",
    )(x_bf, w_attn, cos_t, sin_t, gq, gk)


def _proj_sgu_kernel(x_ref, w_ref, lng_ref, lnb_ref, ws_ref, bs_ref, o_ref):
    acc = jnp.dot(x_ref[...], w_ref[...], preferred_element_type=F32)
    u = _gelu_tanh(acc[:, :SGU_WIDTH])
    v = _gelu_tanh(acc[:, SGU_WIDTH:])
    vn = _layer_norm_rows(v, lng_ref[...], lnb_ref[...]).astype(BF16)
    tm = acc.shape[0]
    for c in range(tm // CHUNK):
        r0 = c * CHUNK
        for g in range(SGU_GROUPS):
            c0 = g * 128
            mixed = jnp.dot(ws_ref[g], vn[r0:r0 + CHUNK, c0:c0 + 128], preferred_element_type=F32)
            mixed = mixed + bs_ref[g]
            o_ref[r0:r0 + CHUNK, c0:c0 + 128] = (u[r0:r0 + CHUNK, c0:c0 + 128] * mixed).astype(o_ref.dtype)


def _proj_sgu(x_bf, w_uv, ln_g, ln_b, w_s, b_s, tm):
    m, d = x_bf.shape
    return pl.pallas_call(
        _proj_sgu_kernel,
        grid=(m // tm,),
        in_specs=[pl.BlockSpec((tm, d), lambda i: (i, 0)),
                  pl.BlockSpec((d, 2 * SGU_WIDTH), lambda i: (0, 0)),
                  pl.BlockSpec((1, SGU_WIDTH), lambda i: (0, 0)),
                  pl.BlockSpec((1, SGU_WIDTH), lambda i: (0, 0)),
                  pl.BlockSpec((SGU_GROUPS, CHUNK, CHUNK), lambda i: (0, 0, 0)),
                  pl.BlockSpec((SGU_GROUPS, CHUNK, 1), lambda i: (0, 0, 0))],
        out_specs=pl.BlockSpec((tm, SGU_WIDTH), lambda i: (i, 0)),
        out_shape=jax.ShapeDtypeStruct((m, SGU_WIDTH), BF16),
        compiler_params=_cparams(("parallel",)),
        name="proj_sgu",
    )(x_bf, w_uv, ln_g, ln_b, w_s, b_s)


def _softmax_pv(s, v):
    mx = jnp.max(s, axis=-1, keepdims=True)
    p = jnp.exp(s - mx)
    l = jnp.sum(p, axis=-1, keepdims=True)
    o = jnp.dot(p.astype(BF16), v, preferred_element_type=F32)
    return o / l


_NT = (((1,), (1,)), ((), ()))


def _gqa_kernel(q_ref, k_ref, v_ref, o_ref):
    k = k_ref[...]
    v = v_ref[...]
    for h in range(Q_PER_KV):
        q = q_ref[:, h * HEAD_DIM:(h + 1) * HEAD_DIM]
        s = lax.dot_general(q, k, _NT, preferred_element_type=F32)
        o_ref[:, h * HEAD_DIM:(h + 1) * HEAD_DIM] = _softmax_pv(s, v).astype(o_ref.dtype)


def _gqa(q, k, v, bsz, seq, tq):
    m = q.shape[0]
    nq = seq // tq
    gw = Q_PER_KV * HEAD_DIM
    return pl.pallas_call(
        _gqa_kernel,
        grid=(bsz, N_KV_HEADS, nq),
        in_specs=[pl.BlockSpec((tq, gw), lambda b, g, i: (b * nq + i, g)),
                  pl.BlockSpec((seq, HEAD_DIM), lambda b, g, i: (b, g)),
                  pl.BlockSpec((seq, HEAD_DIM), lambda b, g, i: (b, g))],
        out_specs=pl.BlockSpec((tq, gw), lambda b, g, i: (b * nq + i, g)),
        out_shape=jax.ShapeDtypeStruct((m, ATTN_WIDTH), BF16),
        compiler_params=_cparams(("parallel", "parallel", "parallel")),
        name="gqa",
    )(q, k, v)


def _mem_attn_kernel(q_ref, kv_ref, o_ref):
    for h in range(MEM_HEADS):
        q = q_ref[:, h * HEAD_DIM:(h + 1) * HEAD_DIM]
        k = kv_ref[:, h * HEAD_DIM:(h + 1) * HEAD_DIM]
        v = kv_ref[:, MEM_WIDTH + h * HEAD_DIM:MEM_WIDTH + (h + 1) * HEAD_DIM]
        s = lax.dot_general(q, k, _NT, preferred_element_type=F32)
        o_ref[:, h * HEAD_DIM:(h + 1) * HEAD_DIM] = _softmax_pv(s, v).astype(o_ref.dtype)


def _mem_attn(qm, kv, bsz, seq, n_mem, tq):
    m = qm.shape[0]
    nq = seq // tq
    return pl.pallas_call(
        _mem_attn_kernel,
        grid=(bsz, nq),
        in_specs=[pl.BlockSpec((tq, MEM_WIDTH), lambda b, i: (b * nq + i, 0)),
                  pl.BlockSpec((n_mem, 2 * MEM_WIDTH), lambda b, i: (b, 0))],
        out_specs=pl.BlockSpec((tq, MEM_WIDTH), lambda b, i: (b * nq + i, 0)),
        out_shape=jax.ShapeDtypeStruct((m, MEM_WIDTH), BF16),
        compiler_params=_cparams(("parallel", "parallel")),
        name="mem_attn",
    )(qm, kv)


def _branch_merge_kernel(x_ref, wg0, wg1, wg2, bg0, bg1, bg2, oa_ref, ob_ref, oc_ref,
                         wba, wbb, wbc, o_ref):
    x = x_ref[...]
    acc = None
    for wg, bg, o, wb in ((wg0, bg0, oa_ref, wba), (wg1, bg1, ob_ref, wbb), (wg2, bg2, oc_ref, wbc)):
        gate = _sigmoid(jnp.dot(x, wg[...], preferred_element_type=F32) + bg[...])
        p = jnp.dot(o[...], wb[...], preferred_element_type=F32)
        acc = gate * p if acc is None else acc + gate * p
    o_ref[...] = acc.astype(o_ref.dtype)


def _branch_merge(x_bf, w_g, b_g, oa, ob, oc, w_branch, tm, tn):
    m, d = x_bf.shape
    nj = d // tn
    row = lambda j, i: (i, 0)
    in_specs = [pl.BlockSpec((tm, d), row)]
    in_specs += [pl.BlockSpec((d, tn), functools.partial(lambda j, i, br: (0, br * nj + j), br=br))
                 for br in range(N_BRANCH)]
    in_specs += [pl.BlockSpec((1, tn), functools.partial(lambda j, i, br: (0, br * nj + j), br=br))
                 for br in range(N_BRANCH)]
    in_specs += [pl.BlockSpec((tm, ATTN_WIDTH), row),
                 pl.BlockSpec((tm, SGU_WIDTH), row),
                 pl.BlockSpec((tm, MEM_WIDTH), row),
                 pl.BlockSpec((ATTN_WIDTH, tn), lambda j, i: (0, j)),
                 pl.BlockSpec((SGU_WIDTH, tn), lambda j, i: (ATTN_WIDTH // SGU_WIDTH, j)),
                 pl.BlockSpec((MEM_WIDTH, tn), lambda j, i: ((ATTN_WIDTH + SGU_WIDTH) // MEM_WIDTH, j))]
    return pl.pallas_call(
        _branch_merge_kernel,
        grid=(nj, m // tm),
        in_specs=in_specs,
        out_specs=pl.BlockSpec((tm, tn), lambda j, i: (i, j)),
        out_shape=jax.ShapeDtypeStruct((m, d), BF16),
        compiler_params=_cparams(("parallel", "parallel")),
        name="branch_merge",
    )(x_bf, w_g, w_g, w_g, b_g, b_g, b_g, oa, ob, oc, w_branch, w_branch, w_branch)


def _split_bf16(x):
    hi = x.astype(BF16)
    lo = (x - hi.astype(F32)).astype(BF16)
    return hi, lo


def _out_ln1_kernel(alpha, mg_ref, wo_ref, x_ref, g_ref, b_ref, wr_ref,
                    x1e_ref, xs_ref, afft_ref):
    tm, d = x_ref.shape
    sub = min(ROW_SUB, tm)
    for r0 in range(0, tm, sub):
        rows = pl.ds(r0, sub)
        y = jnp.dot(mg_ref[rows, :], wo_ref[...], preferred_element_type=F32)
        x1 = _layer_norm_rows(alpha * x_ref[rows, :] + y, g_ref[...], b_ref[...])
        x1e_ref[rows, :d] = x1
        xs_ref[0, rows, :] = alpha * x1
        xs_ref[1, rows, :] = jnp.zeros_like(x1)
        xh, xl = _split_bf16(x1)
        wrh = wr_ref[:, :LANES]
        logits = (jnp.dot(xh, wrh, preferred_element_type=F32)
                  + jnp.dot(xh, wr_ref[:, LANES:], preferred_element_type=F32)
                  + jnp.dot(xl, wrh, preferred_element_type=F32))
        lane = lax.broadcasted_iota(jnp.int32, logits.shape, 1)
        valid = lane < N_EXPERTS
        logits = jnp.where(valid, logits, -1e30)
        mx = jnp.max(logits, axis=-1, keepdims=True)
        e = jnp.where(valid, jnp.exp(logits - mx), 0.0)
        aff = e / jnp.sum(e, axis=-1, keepdims=True)
        x1e_ref[rows, d:] = aff
        afft_ref[:, rows] = aff.T[:N_EXPERTS, :]


def _out_ln1(merged, w_o, x, ln_g, ln_b, wr, alpha, tm):
    m, d = x.shape
    row = lambda i: (i, 0)
    const = lambda i: (0, 0)
    return pl.pallas_call(
        functools.partial(_out_ln1_kernel, alpha),
        grid=(m // tm,),
        in_specs=[pl.BlockSpec((tm, d), row),
                  pl.BlockSpec((d, d), const, pipeline_mode=pl.Buffered(1)),
                  pl.BlockSpec((tm, d), row),
                  pl.BlockSpec((1, d), const),
                  pl.BlockSpec((1, d), const),
                  pl.BlockSpec((d, 2 * LANES), const, pipeline_mode=pl.Buffered(1))],
        out_specs=[pl.BlockSpec((tm, d + LANES), row),
                   pl.BlockSpec((2, tm, d), lambda i: (0, i, 0)),
                   pl.BlockSpec((N_EXPERTS, tm), lambda i: (0, i))],
        out_shape=[jax.ShapeDtypeStruct((m, d + LANES), F32),
                   jax.ShapeDtypeStruct((2, m, d), F32),
                   jax.ShapeDtypeStruct((N_EXPERTS, m), F32)],
        compiler_params=_cparams(("parallel",)),
        name="out_ln1",
    )(merged, w_o, x, ln_g, ln_b, wr)


_BISECT_STEPS = 40


def _route_kernel(cap, a_ref, idx_ref):
    a = a_ref[...]
    n_e, n_r, _ = a.shape
    capf = float(cap)

    def count(mask):
        part = jnp.sum(jnp.where(mask, 1.0, 0.0), axis=2, keepdims=True)
        return jnp.sum(part, axis=1, keepdims=True)

    def bisect(_, carry):
        lo, hi = carry
        mid = 0.5 * (lo + hi)
        ok = count(a >= mid) >= capf
        return jnp.where(ok, mid, lo), jnp.where(ok, hi, mid)

    lo0 = jnp.zeros((n_e, 1, 1), F32)
    hi0 = jnp.full((n_e, 1, 1), 2.0, F32)
    _, hi = lax.fori_loop(0, _BISECT_STEPS, bisect, (lo0, hi0))

    def below_max(hi):
        part = jnp.max(jnp.where(a < hi, a, -1.0), axis=2, keepdims=True)
        return jnp.max(part, axis=1, keepdims=True)

    def refine_cond(carry):
        hi, t = carry
        short = jnp.where(count(a >= t) < capf, 1.0, 0.0)
        return jnp.max(short) > 0.0

    def refine_body(carry):
        hi, t = carry
        hi = jnp.where(count(a >= t) < capf, t, hi)
        return hi, below_max(hi)

    _, thr = lax.while_loop(refine_cond, refine_body, (hi, below_max(hi)))

    ri = lax.broadcasted_iota(jnp.int32, (LANES, LANES), 0)
    ci = lax.broadcasted_iota(jnp.int32, (LANES, LANES), 1)
    tri = jnp.where(ri <= ci, 1.0, 0.0).astype(BF16)
    rr = lax.broadcasted_iota(jnp.int32, (n_r, n_r), 0)
    rc = lax.broadcasted_iota(jnp.int32, (n_r, n_r), 1)
    lstrict = jnp.where(rc < rr, 1.0, 0.0).astype(BF16)

    def prefix_incl(mask_f):
        within = jnp.dot(mask_f.reshape(n_e * n_r, LANES).astype(BF16), tri,
                         preferred_element_type=F32).reshape(n_e, n_r, LANES)
        outs = []
        for e in range(n_e):
            off = jnp.dot(lstrict, within[e].astype(BF16), preferred_element_type=F32)
            outs.append(within[e] + off[:, LANES - 1:LANES])
        return outs

    gt = a > thr
    eq = a == thr
    need = capf - count(gt)
    eq_f = jnp.where(eq, 1.0, 0.0)
    eq_incl = prefix_incl(eq_f)
    j_row = lax.broadcasted_iota(jnp.int32, (1, cap), 1).astype(F32)
    r_col = lax.broadcasted_iota(jnp.int32, (n_r, 1), 0).astype(F32)
    sel_list = []
    for e in range(n_e):
        eq_excl = eq_incl[e] - eq_f[e]
        sel_list.append(jnp.where(gt[e] | (eq[e] & (eq_excl < need[e])), 1.0, 0.0))
    slot_incl = prefix_incl(jnp.stack(sel_list, axis=0))
    for e in range(n_e):
        s_e = slot_incl[e]
        c_e = s_e[:, LANES - 1:LANES]
        blk = jnp.sum(jnp.where(c_e <= j_row, 1.0, 0.0), axis=0, keepdims=True)
        onehot_t = jnp.where(r_col == blk, 1.0, 0.0).astype(BF16)
        s_hi = jnp.floor(s_e * (1.0 / 64.0))
        s_lo = s_e - 64.0 * s_hi
        rows_t = (64.0 * jnp.dot(s_hi.T.astype(BF16), onehot_t, preferred_element_type=F32)
                  + jnp.dot(s_lo.T.astype(BF16), onehot_t, preferred_element_type=F32))
        within = jnp.sum(jnp.where(rows_t <= j_row, 1.0, 0.0), axis=0, keepdims=True)
        idx_ref[e:e + 1, :] = (blk * float(LANES) + within).astype(jnp.int32)


def _route(aff3, cap):
    n_e, n_r, _ = aff3.shape
    return pl.pallas_call(
        functools.partial(_route_kernel, cap),
        grid=(1,),
        in_specs=[pl.BlockSpec((n_e, n_r, LANES), lambda i: (0, 0, 0))],
        out_specs=pl.BlockSpec((n_e, cap), lambda i: (0, 0)),
        out_shape=jax.ShapeDtypeStruct((n_e, cap), jnp.int32),
        compiler_params=_cparams(("arbitrary",)),
        name="route",
    )(aff3)


def _row_copy(src_hbm, dst_vmem, tok, r, sem):
    return pltpu.make_async_copy(src_hbm.at[pl.ds(tok, 1), :], dst_vmem.at[pl.ds(r, 1), :], sem)


def _row_copy_back(src_vmem, dst_hbm, tok, r, sem):
    return pltpu.make_async_copy(src_vmem.at[pl.ds(r, 1), :], dst_hbm.at[pl.ds(tok, 1), :], sem)


def _gather_start(idx_ref, src_hbm, buf, sem, n_rows):
    def body(r, c):
        _row_copy(src_hbm, buf, idx_ref[0, 0, r], r, sem).start()
        return c
    lax.fori_loop(0, n_rows, body, 0, unroll=8)


def _gather_wait(idx_ref, src_hbm, buf, sem, n_rows):
    def body(r, c):
        _row_copy(src_hbm, buf, idx_ref[0, 0, r], r, sem).wait()
        return c
    lax.fori_loop(0, n_rows, body, 0, unroll=8)


def _scatter_start(idx_ref, buf, dst_hbm, sem, n_rows):
    def body(r, c):
        _row_copy_back(buf, dst_hbm, idx_ref[0, 0, r], r, sem).start()
        return c
    lax.fori_loop(0, n_rows, body, 0, unroll=8)


def _scatter_wait(idx_ref, buf, dst_hbm, sem, n_rows):
    def body(r, c):
        _row_copy_back(buf, dst_hbm, idx_ref[0, 0, r], r, sem).wait()
        return c
    lax.fori_loop(0, n_rows, body, 0, unroll=8)


def _moe_gather_kernel(tiles_per_expert, d, idx_ref, idxn_ref, x1e_hbm, xe_ref, g_ref, bufs, sems):
    tm = bufs.shape[1]
    s = pl.program_id(0)
    last = pl.num_programs(0) - 1
    slot = s % 2
    cur = bufs.at[slot]
    nxt = bufs.at[1 - slot]

    @pl.when(s == 0)
    def _():
        _gather_start(idx_ref, x1e_hbm, cur, sems.at[slot], tm)

    @pl.when(s < last)
    def _():
        for r in range(tm):
            _row_copy(x1e_hbm, nxt, idxn_ref[0, 0, r], r, sems.at[1 - slot]).start()

    for r in range(tm):
        _row_copy(x1e_hbm, cur, idx_ref[0, 0, r], r, sems.at[slot]).wait()
    e = s // tiles_per_expert
    xe_ref[...] = cur[:, :d].astype(xe_ref.dtype)
    aff = cur[:, d:]
    lane = lax.broadcasted_iota(jnp.int32, aff.shape, 1)
    g = jnp.sum(jnp.where(lane == e, aff, 0.0), axis=-1, keepdims=True)
    g_ref[...] = jnp.broadcast_to(g, g_ref.shape)


def _moe_gather(idx3, x1e, cap, tm):
    n_tiles = idx3.shape[0]
    d = x1e.shape[1] - LANES
    rows = n_tiles * tm
    return pl.pallas_call(
        functools.partial(_moe_gather_kernel, cap // tm, d),
        grid=(n_tiles,),
        in_specs=[pl.BlockSpec((1, 1, tm), lambda i: (i, 0, 0), memory_space=pltpu.SMEM),
                  pl.BlockSpec((1, 1, tm), lambda i: (jnp.minimum(i + 1, n_tiles - 1), 0, 0),
                               memory_space=pltpu.SMEM),
                  pl.BlockSpec(memory_space=pl.ANY)],
        out_specs=[pl.BlockSpec((tm, d), lambda i: (i, 0)),
                   pl.BlockSpec((tm, LANES), lambda i: (i, 0))],
        out_shape=[jax.ShapeDtypeStruct((rows, d), BF16),
                   jax.ShapeDtypeStruct((rows, LANES), F32)],
        scratch_shapes=[pltpu.VMEM((2, tm, d + LANES), F32), pltpu.SemaphoreType.DMA((2,))],
        compiler_params=_cparams(("arbitrary",)),
        name="moe_gather",
    )(idx3, idx3, x1e)


def _moe_up_kernel(cast_wd, x_ref, wg_ref, wu_ref, *rest):
    x = x_ref[...]
    a = jnp.dot(x, wg_ref[...].astype(BF16), preferred_element_type=F32)
    b = jnp.dot(x, wu_ref[...].astype(BF16), preferred_element_type=F32)
    if cast_wd:
        wd_ref, h_ref, wd_out_ref = rest
        @pl.when(pl.program_id(2) == 0)
        def _():
            wd_out_ref[...] = wd_ref[...].astype(wd_out_ref.dtype)
    else:
        h_ref, = rest
    h_ref[...] = (a * _sigmoid(a) * b).astype(h_ref.dtype)


def _moe_up(xe, w_gate, w_up, w_down, layer, cap, tm, tf):
    rows, d = xe.shape
    _, n_e, _, ff = w_gate.shape
    per_e = cap // tm
    cast_wd = w_down is not None
    in_specs = [pl.BlockSpec((tm, d), lambda e, f, i: (e * per_e + i, 0)),
                pl.BlockSpec((None, None, d, tf), lambda e, f, i: (layer, e, 0, f)),
                pl.BlockSpec((None, None, d, tf), lambda e, f, i: (layer, e, 0, f))]
    out_specs = [pl.BlockSpec((tm, tf), lambda e, f, i: (e * per_e + i, f))]
    out_shape = [jax.ShapeDtypeStruct((rows, ff), BF16)]
    args = [xe, w_gate, w_up]
    if cast_wd:
        in_specs.append(pl.BlockSpec((None, None, tf, d), lambda e, f, i: (layer, e, f, 0)))
        out_specs.append(pl.BlockSpec((None, tf, d), lambda e, f, i: (e, f, 0)))
        out_shape.append(jax.ShapeDtypeStruct((n_e, ff, d), BF16))
        args.append(w_down)
    res = pl.pallas_call(
        functools.partial(_moe_up_kernel, cast_wd),
        grid=(n_e, ff // tf, per_e),
        in_specs=in_specs,
        out_specs=out_specs,
        out_shape=out_shape,
        compiler_params=_cparams(("parallel", "parallel", "arbitrary")),
        name="moe_up_cast" if cast_wd else "moe_up",
    )(*args)
    return res if cast_wd else res[0]


def _moe_down_kernel(per_e, n_tok, idx_ref, idxp_ref, idxpp_ref, h_ref, wd_ref, g_ref, acc_in_hbm, acc_hbm,
                     bufs, gsem, ssems):
    del acc_in_hbm
    tm = bufs.shape[1]
    e = pl.program_id(0)
    s = e * per_e + pl.program_id(1)
    n_steps = pl.num_programs(0) * per_e
    cur = bufs.at[s % 3]
    prv = bufs.at[(s + 2) % 3]
    prv2 = bufs.at[(s + 1) % 3]
    sem_prv = ssems.at[(s + 1) % 2]
    sem_prv2 = ssems.at[s % 2]
    off_cur = (e % 2) * n_tok
    s_prev = (s + n_steps - 1) % n_steps
    off_prv = ((s_prev // per_e) % 2) * n_tok
    s_prev2 = (s + n_steps - 2) % n_steps
    off_prv2 = ((s_prev2 // per_e) % 2) * n_tok

    @pl.when(s == 0)
    def _():
        def start(r, c):
            _row_copy(acc_hbm, prv, off_prv + idxp_ref[0, 0, r], r, gsem).start()
            return c
        lax.fori_loop(0, tm, start, 0, unroll=8)

        def wait(r, c):
            _row_copy(acc_hbm, prv, off_prv + idxp_ref[0, 0, r], r, gsem).wait()
            return c
        lax.fori_loop(0, tm, wait, 0, unroll=8)

    for r in range(tm):
        _row_copy(acc_hbm, cur, off_cur + idx_ref[0, 0, r], r, gsem).start()
    for r in range(tm):
        _row_copy_back(prv, acc_hbm, off_prv + idxp_ref[0, 0, r], r, sem_prv).start()
    ye = jnp.dot(h_ref[...], wd_ref[...], preferred_element_type=F32) * g_ref[:, 0:1]
    for r in range(tm):
        _row_copy(acc_hbm, cur, off_cur + idx_ref[0, 0, r], r, gsem).wait()
    cur[...] = cur[...] + ye

    @pl.when(s > 0)
    def _():
        for r in range(tm):
            _row_copy_back(prv2, acc_hbm, off_prv2 + idxpp_ref[0, 0, r], r, sem_prv2).wait()

    @pl.when(s == n_steps - 1)
    def _():
        def start(r, c):
            _row_copy_back(cur, acc_hbm, off_cur + idx_ref[0, 0, r], r, sem_prv2).start()
            return c
        lax.fori_loop(0, tm, start, 0, unroll=8)

        def wait_prv(r, c):
            _row_copy_back(prv, acc_hbm, off_prv + idxp_ref[0, 0, r], r, sem_prv).wait()
            return c
        lax.fori_loop(0, tm, wait_prv, 0, unroll=8)

        def wait_cur(r, c):
            _row_copy_back(cur, acc_hbm, off_cur + idx_ref[0, 0, r], r, sem_prv2).wait()
            return c
        lax.fori_loop(0, tm, wait_cur, 0, unroll=8)


def _moe_down(idx3, h, w_down, g, acc, cap, tm):
    rows, ff = h.shape
    n_e, _, d = w_down.shape
    per_e = cap // tm
    assert per_e >= 2, "the deferred write-back needs two row tiles per expert"
    n_steps = n_e * per_e
    n_tok = acc.shape[0] // 2
    blk = lambda e, i: e * per_e + i
    return pl.pallas_call(
        functools.partial(_moe_down_kernel, per_e, n_tok),
        grid=(n_e, per_e),
        in_specs=[pl.BlockSpec((1, 1, tm), lambda e, i: (blk(e, i), 0, 0), memory_space=pltpu.SMEM),
                  pl.BlockSpec((1, 1, tm), lambda e, i: ((blk(e, i) + n_steps - 1) % n_steps, 0, 0),
                               memory_space=pltpu.SMEM),
                  pl.BlockSpec((1, 1, tm), lambda e, i: ((blk(e, i) + n_steps - 2) % n_steps, 0, 0),
                               memory_space=pltpu.SMEM),
                  pl.BlockSpec((tm, ff), lambda e, i: (blk(e, i), 0)),
                  pl.BlockSpec((None, ff, d), lambda e, i: (e, 0, 0), pipeline_mode=pl.Buffered(1)),
                  pl.BlockSpec((tm, LANES), lambda e, i: (blk(e, i), 0)),
                  pl.BlockSpec(memory_space=pl.ANY)],
        out_specs=pl.BlockSpec(memory_space=pl.ANY),
        out_shape=jax.ShapeDtypeStruct(acc.shape, acc.dtype),
        scratch_shapes=[pltpu.VMEM((3, tm, d), F32), pltpu.SemaphoreType.DMA(()),
                        pltpu.SemaphoreType.DMA((2,))],
        input_output_aliases={6: 0},
        compiler_params=_cparams(("arbitrary", "arbitrary")),
        name="moe_down",
    )(idx3, idx3, idx3, h, w_down, g, acc)


def _ln2_kernel(x_ref, g_ref, b_ref, y_ref, ybf_ref):
    y = _layer_norm_rows(x_ref[0] + x_ref[1], g_ref[...], b_ref[...])
    y_ref[...] = y
    ybf_ref[...] = y.astype(ybf_ref.dtype)


def _ln2(x, g, b, tm):
    _, m, d = x.shape
    row = lambda i: (i, 0)
    return pl.pallas_call(
        _ln2_kernel,
        grid=(m // tm,),
        in_specs=[pl.BlockSpec((2, tm, d), lambda i: (0, i, 0)),
                  pl.BlockSpec((1, d), lambda i: (0, 0)),
                  pl.BlockSpec((1, d), lambda i: (0, 0))],
        out_specs=[pl.BlockSpec((tm, d), row), pl.BlockSpec((tm, d), row)],
        out_shape=[jax.ShapeDtypeStruct((m, d), F32), jax.ShapeDtypeStruct((m, d), BF16)],
        compiler_params=_cparams(("parallel",)),
        name="ln2",
    )(x, g, b)


def _rope_tables(seq):
    rows = seq // GRID_W
    row = jnp.repeat(jnp.arange(rows), GRID_W).astype(F32)
    col = (jnp.arange(rows * GRID_W) % GRID_W).astype(F32)
    inv = ROPE_THETA ** (-jnp.arange(ROPE_FREQS, dtype=F32) / ROPE_FREQS)
    ang_r = row[:, None] * inv
    ang_c = col[:, None] * inv
    cos_t = jnp.concatenate([jnp.cos(ang_r), jnp.cos(ang_r), jnp.cos(ang_c), jnp.cos(ang_c)], axis=1)
    sin_t = jnp.concatenate([-jnp.sin(ang_r), jnp.sin(ang_r), -jnp.sin(ang_c), jnp.sin(ang_c)], axis=1)
    return cos_t, sin_t


def _prep_layer(l, w_in, b_gate, q_norm_g, k_norm_g, sgu_ln_g, sgu_ln_b, w_s, b_s, w_mem_kv, w_branch,
                w_o, ln1_g, ln1_b, w_router, w_gate, w_up, w_down, ln2_g, ln2_b):
    d = w_in.shape[1]
    wi = w_in[l]
    c_q, c_k, c_v = ATTN_WIDTH, ATTN_WIDTH + KV_WIDTH, ATTN_WIDTH + 2 * KV_WIDTH
    c_u = c_v + SGU_WIDTH
    c_vb = c_u + SGU_WIDTH
    c_qm = c_vb + MEM_WIDTH
    wr = jnp.pad(w_router[l], ((0, 0), (0, LANES - N_EXPERTS)))
    wr_hi = wr.astype(BF16)
    wr_lo = (wr - wr_hi.astype(F32)).astype(BF16)
    return dict(
        w_attn=jnp.concatenate([wi[:, :c_v], wi[:, c_vb:c_qm]], axis=1).astype(BF16),
        w_uv=wi[:, c_v:c_vb].astype(BF16),
        w_g=wi[:, c_qm:].astype(BF16),
        b_g=b_gate[l].reshape(1, N_BRANCH * d),
        gq=q_norm_g[l].reshape(1, HEAD_DIM), gk=k_norm_g[l].reshape(1, HEAD_DIM),
        sgu_g=sgu_ln_g[l].reshape(1, SGU_WIDTH), sgu_b=sgu_ln_b[l].reshape(1, SGU_WIDTH),
        w_s=w_s[l].astype(BF16), b_s=b_s[l].reshape(SGU_GROUPS, CHUNK, 1),
        w_mem_kv=w_mem_kv[l].astype(BF16), w_branch=w_branch[l].astype(BF16), w_o=w_o[l].astype(BF16),
        ln1_g=ln1_g[l].reshape(1, d), ln1_b=ln1_b[l].reshape(1, d),
        wr=jnp.concatenate([wr_hi, wr_lo], axis=1),
        layer=l, w_gate=w_gate, w_up=w_up, w_down=w_down,
        ln2_g=ln2_g[l].reshape(1, d), ln2_b=ln2_b[l].reshape(1, d),
    )


def _run_trunk(x, mem, layers, alpha, wd_bf16):
    bsz, seq, d = x.shape
    n_mem = mem.shape[1]
    m = bsz * seq
    cap = EC_FACTOR * m // N_EXPERTS
    cos_t, sin_t = _rope_tables(seq)
    xf = x.reshape(m, d)
    x_bf = xf.astype(BF16)
    mem_bf = mem.reshape(bsz * n_mem, d).astype(BF16)
    tm_moe = min(512, cap // 2)
    for p in layers:
        q, k, v, qm = _proj_attn(x_bf, p["w_attn"], cos_t, sin_t, p["gq"], p["gk"], seq, tm=512)
        out_b = _proj_sgu(x_bf, p["w_uv"], p["sgu_g"], p["sgu_b"], p["w_s"], p["b_s"], tm=512)
        out_a = _gqa(q, k, v, bsz, seq, tq=256)
        kv = _mm_plain(mem_bf, p["w_mem_kv"], tm=min(512, bsz * n_mem))
        out_c = _mem_attn(qm, kv, bsz, seq, n_mem, tq=512)
        merged = _branch_merge(x_bf, p["w_g"], p["b_g"], out_a, out_b, out_c, p["w_branch"], tm=512, tn=512)
        x1e, acc, aff_t = _out_ln1(merged, p["w_o"], xf, p["ln1_g"], p["ln1_b"], p["wr"], alpha, tm=512)
        idx = _route(aff_t.reshape(N_EXPERTS, m // LANES, LANES), cap)
        idx3 = idx.reshape(N_EXPERTS * cap // tm_moe, 1, tm_moe)
        xe, g = _moe_gather(idx3, x1e, cap, tm_moe)
        cast_wd = p["layer"] not in wd_bf16
        res = _moe_up(xe, p["w_gate"], p["w_up"], p["w_down"] if cast_wd else None, p["layer"], cap,
                      tm=min(1024, cap), tf=min(512, p["w_gate"].shape[3]))
        if cast_wd:
            h, wd_bf16[p["layer"]] = res
        else:
            h = res
        acc = _moe_down(idx3, h, wd_bf16[p["layer"]], g, acc.reshape(2 * m, d), cap, tm_moe)
        xf, x_bf = _ln2(acc.reshape(2, m, d), p["ln2_g"], p["ln2_b"], tm=512)
    return xf.reshape(bsz, seq, d)


def kernel(x_prompt, x_sample, mem_prompt, mem_sample, w_in, b_gate, q_norm_g, k_norm_g, sgu_ln_g, sgu_ln_b,
           w_s, b_s, w_mem_kv, w_branch, w_o, ln1_g, ln1_b, w_router, w_gate, w_up, w_down, ln2_g, ln2_b):
    depth = w_in.shape[0]
    alpha = (2 * depth) ** 0.25
    layers = [_prep_layer(l, w_in, b_gate, q_norm_g, k_norm_g, sgu_ln_g, sgu_ln_b, w_s, b_s, w_mem_kv,
                          w_branch, w_o, ln1_g, ln1_b, w_router, w_gate, w_up, w_down, ln2_g, ln2_b)
              for l in range(depth)]
    wd_bf16 = {}
    y_sample = _run_trunk(x_sample, mem_sample, layers, alpha, wd_bf16)
    y_prompt = _run_trunk(x_prompt, mem_prompt, layers, alpha, wd_bf16)
    return (y_prompt, y_sample)
```

```python
import functools

import jax
import jax.numpy as jnp
from jax import lax
from jax.experimental import pallas as pl
from jax.experimental.pallas import tpu as pltpu

F32 = jnp.float32
BF16 = jnp.bfloat16

HEAD_DIM = 128
N_Q_HEADS = 8
N_KV_HEADS = 2
Q_PER_KV = N_Q_HEADS // N_KV_HEADS
ATTN_WIDTH = N_Q_HEADS * HEAD_DIM
KV_WIDTH = N_KV_HEADS * HEAD_DIM
SGU_GROUPS = 8
SGU_WIDTH = SGU_GROUPS * 128
CHUNK = 128
MEM_HEADS = 4
MEM_WIDTH = MEM_HEADS * HEAD_DIM
N_BRANCH = 3
N_EXPERTS = 16
EC_FACTOR = 2
GRID_W = 64
ROPE_THETA = 10000.0
ROPE_FREQS = HEAD_DIM // 4
EPS = 1e-6
LANES = 128
ROW_SUB = 256
VMEM_LIMIT = 56 * 1024 * 1024


def _cparams(sem, vmem=VMEM_LIMIT):
    return pltpu.CompilerParams(dimension_semantics=sem, vmem_limit_bytes=vmem)


def _gelu_tanh(x):
    return 0.5 * x * (1.0 + jnp.tanh(0.7978845608028654 * (x + 0.044715 * (x * x * x))))


def _sigmoid(x):
    return 1.0 / (1.0 + jnp.exp(-x))


def _layer_norm_rows(x, g, b):
    mu = jnp.mean(x, axis=-1, keepdims=True)
    xc = x - mu
    var = jnp.mean(xc * xc, axis=-1, keepdims=True)
    return xc * lax.rsqrt(var + EPS) * g + b


def _mm_plain_kernel(x_ref, w_ref, o_ref):
    o_ref[...] = jnp.dot(x_ref[...], w_ref[...], preferred_element_type=F32).astype(o_ref.dtype)


def _mm_plain(x, w, tm):
    m, k = x.shape
    n = w.shape[1]
    return pl.pallas_call(
        _mm_plain_kernel,
        grid=(m // tm,),
        in_specs=[pl.BlockSpec((tm, k), lambda i: (i, 0)),
                  pl.BlockSpec((k, n), lambda i: (0, 0))],
        out_specs=pl.BlockSpec((tm, n), lambda i: (i, 0)),
        out_shape=jax.ShapeDtypeStruct((m, n), BF16),
        compiler_params=_cparams(("parallel",)),
        name="mm_plain",
    )(x, w)


def _proj_attn_kernel(x_ref, w_ref, cos_ref, sin_ref, gq_ref, gk_ref, q_ref, k_ref, v_ref, qm_ref):
    scale = HEAD_DIM ** -0.5
    gq = gq_ref[...]
    gk = gk_ref[...]
    tm = x_ref.shape[0]
    for r0 in range(0, tm, ROW_SUB):
        rows = pl.ds(r0, ROW_SUB)
        acc = jnp.dot(x_ref[rows, :], w_ref[...], preferred_element_type=F32)
        cos = cos_ref[rows, :]
        sin = sin_ref[rows, :]
        lane = lax.broadcasted_iota(jnp.int32, cos.shape, 1)
        first_half = (lane % (2 * ROPE_FREQS)) < ROPE_FREQS

        def norm_rope(blk, g):
            ms = jnp.mean(blk * blk, axis=-1, keepdims=True)
            y = blk * lax.rsqrt(ms + EPS) * g
            partner = jnp.where(first_half,
                                pltpu.roll(y, HEAD_DIM - ROPE_FREQS, 1),
                                pltpu.roll(y, ROPE_FREQS, 1))
            return y * cos + partner * sin

        for h in range(N_Q_HEADS):
            blk = acc[:, h * HEAD_DIM:(h + 1) * HEAD_DIM]
            q_ref[rows, h * HEAD_DIM:(h + 1) * HEAD_DIM] = (norm_rope(blk, gq) * scale).astype(q_ref.dtype)
        for h in range(N_KV_HEADS):
            c0 = ATTN_WIDTH + h * HEAD_DIM
            k_ref[rows, h * HEAD_DIM:(h + 1) * HEAD_DIM] = norm_rope(acc[:, c0:c0 + HEAD_DIM], gk).astype(k_ref.dtype)
        c0 = ATTN_WIDTH + KV_WIDTH
        v_ref[rows, :] = acc[:, c0:c0 + KV_WIDTH].astype(v_ref.dtype)
        c0 += KV_WIDTH
        qm_ref[rows, :] = (acc[:, c0:c0 + MEM_WIDTH] * scale).astype(qm_ref.dtype)


def _proj_attn(x_bf, w_attn, cos_t, sin_t, gq, gk, seq, tm):
    m, d = x_bf.shape
    n = w_attn.shape[1]
    per_seq = seq // tm
    row = lambda i: (i, 0)
    const = lambda i: (0, 0)
    return pl.pallas_call(
        _proj_attn_kernel,
        grid=(m // tm,),
        in_specs=[pl.BlockSpec((tm, d), row),
                  pl.BlockSpec((d, n), const),
                  pl.BlockSpec((tm, HEAD_DIM), lambda i: (i % per_seq, 0)),
                  pl.BlockSpec((tm, HEAD_DIM), lambda i: (i % per_seq, 0)),
                  pl.BlockSpec((1, HEAD_DIM), const),
                  pl.BlockSpec((1, HEAD_DIM), const)],
        out_specs=[pl.BlockSpec((tm, ATTN_WIDTH), row),
                   pl.BlockSpec((tm, KV_WIDTH), row),
                   pl.BlockSpec((tm, KV_WIDTH), row),
                   pl.BlockSpec((tm, MEM_WIDTH), row)],
        out_shape=[jax.ShapeDtypeStruct((m, ATTN_WIDTH), BF16),
                   jax.ShapeDtypeStruct((m, KV_WIDTH), BF16),
                   jax.ShapeDtypeStruct((m, KV_WIDTH), BF16),
                   jax.ShapeDtypeStruct((m, MEM_WIDTH), BF16)],
        compiler_params=_cparams(("parallel",)),
        name="proj_attn",
    )(x_bf, w_attn, cos_t, sin_t, gq, gk)


def _proj_sgu_kernel(x_ref, w_ref, lng_ref, lnb_ref, ws_ref, bs_ref, o_ref):
    acc = jnp.dot(x_ref[...], w_ref[...], preferred_element_type=F32)
    u = _gelu_tanh(acc[:, :SGU_WIDTH])
    v = _gelu_tanh(acc[:, SGU_WIDTH:])
    vn = _layer_norm_rows(v, lng_ref[...], lnb_ref[...]).astype(BF16)
    tm = acc.shape[0]
    for c in range(tm // CHUNK):
        r0 = c * CHUNK
        for g in range(SGU_GROUPS):
            c0 = g * 128
            mixed = jnp.dot(ws_ref[g], vn[r0:r0 + CHUNK, c0:c0 + 128], preferred_element_type=F32)
            mixed = mixed + bs_ref[g]
            o_ref[r0:r0 + CHUNK, c0:c0 + 128] = (u[r0:r0 + CHUNK, c0:c0 + 128] * mixed).astype(o_ref.dtype)


def _proj_sgu(x_bf, w_uv, ln_g, ln_b, w_s, b_s, tm):
    m, d = x_bf.shape
    return pl.pallas_call(
        _proj_sgu_kernel,
        grid=(m // tm,),
        in_specs=[pl.BlockSpec((tm, d), lambda i: (i, 0)),
                  pl.BlockSpec((d, 2 * SGU_WIDTH), lambda i: (0, 0)),
                  pl.BlockSpec((1, SGU_WIDTH), lambda i: (0, 0)),
                  pl.BlockSpec((1, SGU_WIDTH), lambda i: (0, 0)),
                  pl.BlockSpec((SGU_GROUPS, CHUNK, CHUNK), lambda i: (0, 0, 0)),
                  pl.BlockSpec((SGU_GROUPS, CHUNK, 1), lambda i: (0, 0, 0))],
        out_specs=pl.BlockSpec((tm, SGU_WIDTH), lambda i: (i, 0)),
        out_shape=jax.ShapeDtypeStruct((m, SGU_WIDTH), BF16),
        compiler_params=_cparams(("parallel",)),
        name="proj_sgu",
    )(x_bf, w_uv, ln_g, ln_b, w_s, b_s)


def _softmax_pv(s, v):
    mx = jnp.max(s, axis=-1, keepdims=True)
    p = jnp.exp(s - mx)
    l = jnp.sum(p, axis=-1, keepdims=True)
    o = jnp.dot(p.astype(BF16), v, preferred_element_type=F32)
    return o / l


_NT = (((1,), (1,)), ((), ()))


def _gqa_kernel(q_ref, k_ref, v_ref, o_ref):
    k = k_ref[...]
    v = v_ref[...]
    for h in range(Q_PER_KV):
        q = q_ref[:, h * HEAD_DIM:(h + 1) * HEAD_DIM]
        s = lax.dot_general(q, k, _NT, preferred_element_type=F32)
        o_ref[:, h * HEAD_DIM:(h + 1) * HEAD_DIM] = _softmax_pv(s, v).astype(o_ref.dtype)


def _gqa(q, k, v, bsz, seq, tq):
    m = q.shape[0]
    nq = seq // tq
    gw = Q_PER_KV * HEAD_DIM
    return pl.pallas_call(
        _gqa_kernel,
        grid=(bsz, N_KV_HEADS, nq),
        in_specs=[pl.BlockSpec((tq, gw), lambda b, g, i: (b * nq + i, g)),
                  pl.BlockSpec((seq, HEAD_DIM), lambda b, g, i: (b, g)),
                  pl.BlockSpec((seq, HEAD_DIM), lambda b, g, i: (b, g))],
        out_specs=pl.BlockSpec((tq, gw), lambda b, g, i: (b * nq + i, g)),
        out_shape=jax.ShapeDtypeStruct((m, ATTN_WIDTH), BF16),
        compiler_params=_cparams(("parallel", "parallel", "parallel")),
        name="gqa",
    )(q, k, v)


def _mem_attn_kernel(q_ref, kv_ref, o_ref):
    for h in range(MEM_HEADS):
        q = q_ref[:, h * HEAD_DIM:(h + 1) * HEAD_DIM]
        k = kv_ref[:, h * HEAD_DIM:(h + 1) * HEAD_DIM]
        v = kv_ref[:, MEM_WIDTH + h * HEAD_DIM:MEM_WIDTH + (h + 1) * HEAD_DIM]
        s = lax.dot_general(q, k, _NT, preferred_element_type=F32)
        o_ref[:, h * HEAD_DIM:(h + 1) * HEAD_DIM] = _softmax_pv(s, v).astype(o_ref.dtype)


def _mem_attn(qm, kv, bsz, seq, n_mem, tq):
    m = qm.shape[0]
    nq = seq // tq
    return pl.pallas_call(
        _mem_attn_kernel,
        grid=(bsz, nq),
        in_specs=[pl.BlockSpec((tq, MEM_WIDTH), lambda b, i: (b * nq + i, 0)),
                  pl.BlockSpec((n_mem, 2 * MEM_WIDTH), lambda b, i: (b, 0))],
        out_specs=pl.BlockSpec((tq, MEM_WIDTH), lambda b, i: (b * nq + i, 0)),
        out_shape=jax.ShapeDtypeStruct((m, MEM_WIDTH), BF16),
        compiler_params=_cparams(("parallel", "parallel")),
        name="mem_attn",
    )(qm, kv)


def _branch_merge_kernel(x_ref, wg0, wg1, wg2, bg0, bg1, bg2, oa_ref, ob_ref, oc_ref,
                         wba, wbb, wbc, o_ref):
    x = x_ref[...]
    acc = None
    for wg, bg, o, wb in ((wg0, bg0, oa_ref, wba), (wg1, bg1, ob_ref, wbb), (wg2, bg2, oc_ref, wbc)):
        gate = _sigmoid(jnp.dot(x, wg[...], preferred_element_type=F32) + bg[...])
        p = jnp.dot(o[...], wb[...], preferred_element_type=F32)
        acc = gate * p if acc is None else acc + gate * p
    o_ref[...] = acc.astype(o_ref.dtype)


def _branch_merge(x_bf, w_g, b_g, oa, ob, oc, w_branch, tm, tn):
    m, d = x_bf.shape
    nj = d // tn
    row = lambda j, i: (i, 0)
    in_specs = [pl.BlockSpec((tm, d), row)]
    in_specs += [pl.BlockSpec((d, tn), functools.partial(lambda j, i, br: (0, br * nj + j), br=br))
                 for br in range(N_BRANCH)]
    in_specs += [pl.BlockSpec((1, tn), functools.partial(lambda j, i, br: (0, br * nj + j), br=br))
                 for br in range(N_BRANCH)]
    in_specs += [pl.BlockSpec((tm, ATTN_WIDTH), row),
                 pl.BlockSpec((tm, SGU_WIDTH), row),
                 pl.BlockSpec((tm, MEM_WIDTH), row),
                 pl.BlockSpec((ATTN_WIDTH, tn), lambda j, i: (0, j)),
                 pl.BlockSpec((SGU_WIDTH, tn), lambda j, i: (ATTN_WIDTH // SGU_WIDTH, j)),
                 pl.BlockSpec((MEM_WIDTH, tn), lambda j, i: ((ATTN_WIDTH + SGU_WIDTH) // MEM_WIDTH, j))]
    return pl.pallas_call(
        _branch_merge_kernel,
        grid=(nj, m // tm),
        in_specs=in_specs,
        out_specs=pl.BlockSpec((tm, tn), lambda j, i: (i, j)),
        out_shape=jax.ShapeDtypeStruct((m, d), BF16),
        compiler_params=_cparams(("parallel", "parallel")),
        name="branch_merge",
    )(x_bf, w_g, w_g, w_g, b_g, b_g, b_g, oa, ob, oc, w_branch, w_branch, w_branch)


def _split_bf16(x):
    hi = x.astype(BF16)
    lo = (x - hi.astype(F32)).astype(BF16)
    return hi, lo


def _out_ln1_kernel(alpha, mg_ref, wo_ref, x_ref, g_ref, b_ref, wr_ref,
                    x1e_ref, xs_ref, afft_ref):
    tm, d = x_ref.shape
    sub = min(ROW_SUB, tm)
    for r0 in range(0, tm, sub):
        rows = pl.ds(r0, sub)
        y = jnp.dot(mg_ref[rows, :], wo_ref[...], preferred_element_type=F32)
        x1 = _layer_norm_rows(alpha * x_ref[rows, :] + y, g_ref[...], b_ref[...])
        x1e_ref[rows, :d] = x1
        xs_ref[0, rows, :] = alpha * x1
        xs_ref[1, rows, :] = jnp.zeros_like(x1)
        xh, xl = _split_bf16(x1)
        wrh = wr_ref[:, :LANES]
        logits = (jnp.dot(xh, wrh, preferred_element_type=F32)
                  + jnp.dot(xh, wr_ref[:, LANES:], preferred_element_type=F32)
                  + jnp.dot(xl, wrh, preferred_element_type=F32))
        lane = lax.broadcasted_iota(jnp.int32, logits.shape, 1)
        valid = lane < N_EXPERTS
        logits = jnp.where(valid, logits, -1e30)
        mx = jnp.max(logits, axis=-1, keepdims=True)
        e = jnp.where(valid, jnp.exp(logits - mx), 0.0)
        aff = e / jnp.sum(e, axis=-1, keepdims=True)
        x1e_ref[rows, d:] = aff
        afft_ref[:, rows] = aff.T[:N_EXPERTS, :]


def _out_ln1(merged, w_o, x, ln_g, ln_b, wr, alpha, tm):
    m, d = x.shape
    row = lambda i: (i, 0)
    const = lambda i: (0, 0)
    return pl.pallas_call(
        functools.partial(_out_ln1_kernel, alpha),
        grid=(m // tm,),
        in_specs=[pl.BlockSpec((tm, d), row),
                  pl.BlockSpec((d, d), const, pipeline_mode=pl.Buffered(1)),
                  pl.BlockSpec((tm, d), row),
                  pl.BlockSpec((1, d), const),
                  pl.BlockSpec((1, d), const),
                  pl.BlockSpec((d, 2 * LANES), const, pipeline_mode=pl.Buffered(1))],
        out_specs=[pl.BlockSpec((tm, d + LANES), row),
                   pl.BlockSpec((2, tm, d), lambda i: (0, i, 0)),
                   pl.BlockSpec((N_EXPERTS, tm), lambda i: (0, i))],
        out_shape=[jax.ShapeDtypeStruct((m, d + LANES), F32),
                   jax.ShapeDtypeStruct((2, m, d), F32),
                   jax.ShapeDtypeStruct((N_EXPERTS, m), F32)],
        compiler_params=_cparams(("parallel",)),
        name="out_ln1",
    )(merged, w_o, x, ln_g, ln_b, wr)


_BISECT_STEPS = 40


def _route_kernel(cap, a_ref, idx_ref):
    a = a_ref[...]
    n_e, n_r, _ = a.shape
    capf = float(cap)

    def count(mask):
        part = jnp.sum(jnp.where(mask, 1.0, 0.0), axis=2, keepdims=True)
        return jnp.sum(part, axis=1, keepdims=True)

    def bisect(_, carry):
        lo, hi = carry
        mid = 0.5 * (lo + hi)
        ok = count(a >= mid) >= capf
        return jnp.where(ok, mid, lo), jnp.where(ok, hi, mid)

    lo0 = jnp.zeros((n_e, 1, 1), F32)
    hi0 = jnp.full((n_e, 1, 1), 2.0, F32)
    _, hi = lax.fori_loop(0, _BISECT_STEPS, bisect, (lo0, hi0))

    def below_max(hi):
        part = jnp.max(jnp.where(a < hi, a, -1.0), axis=2, keepdims=True)
        return jnp.max(part, axis=1, keepdims=True)

    def refine_cond(carry):
        hi, t = carry
        short = jnp.where(count(a >= t) < capf, 1.0, 0.0)
        return jnp.max(short) > 0.0

    def refine_body(carry):
        hi, t = carry
        hi = jnp.where(count(a >= t) < capf, t, hi)
        return hi, below_max(hi)

    _, thr = lax.while_loop(refine_cond, refine_body, (hi, below_max(hi)))

    ri = lax.broadcasted_iota(jnp.int32, (LANES, LANES), 0)
    ci = lax.broadcasted_iota(jnp.int32, (LANES, LANES), 1)
    tri = jnp.where(ri <= ci, 1.0, 0.0).astype(BF16)
    rr = lax.broadcasted_iota(jnp.int32, (n_r, n_r), 0)
    rc = lax.broadcasted_iota(jnp.int32, (n_r, n_r), 1)
    lstrict = jnp.where(rc < rr, 1.0, 0.0).astype(BF16)

    def prefix_incl(mask_f):
        within = jnp.dot(mask_f.reshape(n_e * n_r, LANES).astype(BF16), tri,
                         preferred_element_type=F32).reshape(n_e, n_r, LANES)
        outs = []
        for e in range(n_e):
            off = jnp.dot(lstrict, within[e].astype(BF16), preferred_element_type=F32)
            outs.append(within[e] + off[:, LANES - 1:LANES])
        return outs

    gt = a > thr
    eq = a == thr
    need = capf - count(gt)
    eq_f = jnp.where(eq, 1.0, 0.0)
    eq_incl = prefix_incl(eq_f)
    j_row = lax.broadcasted_iota(jnp.int32, (1, cap), 1).astype(F32)
    r_col = lax.broadcasted_iota(jnp.int32, (n_r, 1), 0).astype(F32)
    sel_list = []
    for e in range(n_e):
        eq_excl = eq_incl[e] - eq_f[e]
        sel_list.append(jnp.where(gt[e] | (eq[e] & (eq_excl < need[e])), 1.0, 0.0))
    slot_incl = prefix_incl(jnp.stack(sel_list, axis=0))
    for e in range(n_e):
        s_e = slot_incl[e]
        c_e = s_e[:, LANES - 1:LANES]
        blk = jnp.sum(jnp.where(c_e <= j_row, 1.0, 0.0), axis=0, keepdims=True)
        onehot_t = jnp.where(r_col == blk, 1.0, 0.0).astype(BF16)
        s_hi = jnp.floor(s_e * (1.0 / 64.0))
        s_lo = s_e - 64.0 * s_hi
        rows_t = (64.0 * jnp.dot(s_hi.T.astype(BF16), onehot_t, preferred_element_type=F32)
                  + jnp.dot(s_lo.T.astype(BF16), onehot_t, preferred_element_type=F32))
        within = jnp.sum(jnp.where(rows_t <= j_row, 1.0, 0.0), axis=0, keepdims=True)
        idx_ref[e:e + 1, :] = (blk * float(LANES) + within).astype(jnp.int32)


def _route(aff3, cap):
    n_e, n_r, _ = aff3.shape
    return pl.pallas_call(
        functools.partial(_route_kernel, cap),
        grid=(1,),
        in_specs=[pl.BlockSpec((n_e, n_r, LANES), lambda i: (0, 0, 0))],
        out_specs=pl.BlockSpec((n_e, cap), lambda i: (0, 0)),
        out_shape=jax.ShapeDtypeStruct((n_e, cap), jnp.int32),
        compiler_params=_cparams(("arbitrary",)),
        name="route",
    )(aff3)


def _row_copy(src_hbm, dst_vmem, tok, r, sem):
    return pltpu.make_async_copy(src_hbm.at[pl.ds(tok, 1), :], dst_vmem.at[pl.ds(r, 1), :], sem)


def _row_copy_back(src_vmem, dst_hbm, tok, r, sem):
    return pltpu.make_async_copy(src_vmem.at[pl.ds(r, 1), :], dst_hbm.at[pl.ds(tok, 1), :], sem)


def _gather_start(idx_ref, src_hbm, buf, sem, n_rows):
    def body(r, c):
        _row_copy(src_hbm, buf, idx_ref[0, 0, r], r, sem).start()
        return c
    lax.fori_loop(0, n_rows, body, 0, unroll=8)


def _gather_wait(idx_ref, src_hbm, buf, sem, n_rows):
    def body(r, c):
        _row_copy(src_hbm, buf, idx_ref[0, 0, r], r, sem).wait()
        return c
    lax.fori_loop(0, n_rows, body, 0, unroll=8)


def _scatter_start(idx_ref, buf, dst_hbm, sem, n_rows):
    def body(r, c):
        _row_copy_back(buf, dst_hbm, idx_ref[0, 0, r], r, sem).start()
        return c
    lax.fori_loop(0, n_rows, body, 0, unroll=8)


def _scatter_wait(idx_ref, buf, dst_hbm, sem, n_rows):
    def body(r, c):
        _row_copy_back(buf, dst_hbm, idx_ref[0, 0, r], r, sem).wait()
        return c
    lax.fori_loop(0, n_rows, body, 0, unroll=8)


def _moe_gather_kernel(tiles_per_expert, d, idx_ref, idxn_ref, x1e_hbm, xe_ref, g_ref, bufs, sems):
    tm = bufs.shape[1]
    s = pl.program_id(0)
    last = pl.num_programs(0) - 1
    slot = s % 2
    cur = bufs.at[slot]
    nxt = bufs.at[1 - slot]

    @pl.when(s == 0)
    def _():
        _gather_start(idx_ref, x1e_hbm, cur, sems.at[slot], tm)

    @pl.when(s < last)
    def _():
        for r in range(tm):
            _row_copy(x1e_hbm, nxt, idxn_ref[0, 0, r], r, sems.at[1 - slot]).start()

    for r in range(tm):
        _row_copy(x1e_hbm, cur, idx_ref[0, 0, r], r, sems.at[slot]).wait()
    e = s // tiles_per_expert
    xe_ref[...] = cur[:, :d].astype(xe_ref.dtype)
    aff = cur[:, d:]
    lane = lax.broadcasted_iota(jnp.int32, aff.shape, 1)
    g = jnp.sum(jnp.where(lane == e, aff, 0.0), axis=-1, keepdims=True)
    g_ref[...] = jnp.broadcast_to(g, g_ref.shape)


def _moe_gather(idx3, x1e, cap, tm):
    n_tiles = idx3.shape[0]
    d = x1e.shape[1] - LANES
    rows = n_tiles * tm
    return pl.pallas_call(
        functools.partial(_moe_gather_kernel, cap // tm, d),
        grid=(n_tiles,),
        in_specs=[pl.BlockSpec((1, 1, tm), lambda i: (i, 0, 0), memory_space=pltpu.SMEM),
                  pl.BlockSpec((1, 1, tm), lambda i: (jnp.minimum(i + 1, n_tiles - 1), 0, 0),
                               memory_space=pltpu.SMEM),
                  pl.BlockSpec(memory_space=pl.ANY)],
        out_specs=[pl.BlockSpec((tm, d), lambda i: (i, 0)),
                   pl.BlockSpec((tm, LANES), lambda i: (i, 0))],
        out_shape=[jax.ShapeDtypeStruct((rows, d), BF16),
                   jax.ShapeDtypeStruct((rows, LANES), F32)],
        scratch_shapes=[pltpu.VMEM((2, tm, d + LANES), F32), pltpu.SemaphoreType.DMA((2,))],
        compiler_params=_cparams(("arbitrary",)),
        name="moe_gather",
    )(idx3, idx3, x1e)


def _moe_up_kernel(cast_wd, layer, x_ref, wg_hbm, wu_hbm, *rest):
    if cast_wd:
        wd_hbm, h_ref, wd_out_ref, wbuf, wdbuf, sems = rest
    else:
        h_ref, wbuf, sems = rest
    nf = pl.num_programs(1)
    n_pairs = pl.num_programs(0) * nf
    t = pl.program_id(0) * nf + pl.program_id(1)
    slot = t % 2
    tf = wbuf.shape[-1]

    def tile_copies(pair, sl):
        ee = pair // nf
        c0 = pl.multiple_of((pair % nf) * tf, tf)
        cps = [pltpu.make_async_copy(wg_hbm.at[layer, ee, :, pl.ds(c0, tf)], wbuf.at[sl, 0], sems.at[sl, 0]),
               pltpu.make_async_copy(wu_hbm.at[layer, ee, :, pl.ds(c0, tf)], wbuf.at[sl, 1], sems.at[sl, 1])]
        if cast_wd:
            cps.append(pltpu.make_async_copy(wd_hbm.at[layer, ee, pl.ds(c0, tf), :], wdbuf.at[sl],
                                             sems.at[sl, 2]))
        return cps

    @pl.when(pl.program_id(2) == 0)
    def _():
        @pl.when(t == 0)
        def _():
            for c in tile_copies(t, slot):
                c.start()

        @pl.when(t + 1 < n_pairs)
        def _():
            for c in tile_copies(t + 1, 1 - slot):
                c.start()

        for c in tile_copies(t, slot):
            c.wait()
        if cast_wd:
            wd_out_ref[...] = wdbuf[slot].astype(wd_out_ref.dtype)

    x = x_ref[...]
    a = jnp.dot(x, wbuf[slot, 0].astype(BF16), preferred_element_type=F32)
    b = jnp.dot(x, wbuf[slot, 1].astype(BF16), preferred_element_type=F32)
    h_ref[...] = (a * _sigmoid(a) * b).astype(h_ref.dtype)


def _moe_up(xe, w_gate, w_up, w_down, layer, cap, tm, tf):
    rows, d = xe.shape
    _, n_e, _, ff = w_gate.shape
    per_e = cap // tm
    cast_wd = w_down is not None
    hbm = pl.BlockSpec(memory_space=pl.ANY)
    in_specs = [pl.BlockSpec((tm, d), lambda e, f, i: (e * per_e + i, 0)), hbm, hbm]
    out_specs = [pl.BlockSpec((tm, tf), lambda e, f, i: (e * per_e + i, f))]
    out_shape = [jax.ShapeDtypeStruct((rows, ff), BF16)]
    scratch = [pltpu.VMEM((2, 2, d, tf), F32)]
    args = [xe, w_gate, w_up]
    if cast_wd:
        in_specs.append(hbm)
        out_specs.append(pl.BlockSpec((None, tf, d), lambda e, f, i: (e, f, 0)))
        out_shape.append(jax.ShapeDtypeStruct((n_e, ff, d), BF16))
        scratch.append(pltpu.VMEM((2, tf, d), F32))
        args.append(w_down)
    scratch.append(pltpu.SemaphoreType.DMA((2, 3 if cast_wd else 2)))
    res = pl.pallas_call(
        functools.partial(_moe_up_kernel, cast_wd, layer),
        grid=(n_e, ff // tf, per_e),
        in_specs=in_specs,
        out_specs=out_specs,
        out_shape=out_shape,
        scratch_shapes=scratch,
        compiler_params=_cparams(("arbitrary", "arbitrary", "arbitrary")),
        name="moe_up_cast" if cast_wd else "moe_up",
    )(*args)
    return res if cast_wd else res[0]


def _moe_down_kernel(per_e, n_tok, idx_ref, idxp_ref, h_ref, wd_ref, g_ref, acc_in_hbm, acc_hbm,
                     bufs, gsem, ssem):
    del acc_in_hbm
    tm = bufs.shape[1]
    e = pl.program_id(0)
    s = e * per_e + pl.program_id(1)
    n_steps = pl.num_programs(0) * per_e
    slot = s % 2
    cur = bufs.at[slot]
    prv = bufs.at[1 - slot]
    off_cur = (e % 2) * n_tok
    s_prev = (s + n_steps - 1) % n_steps
    off_prv = ((s_prev // per_e) % 2) * n_tok

    @pl.when(s == 0)
    def _():
        def start(r, c):
            _row_copy(acc_hbm, prv, off_prv + idxp_ref[0, 0, r], r, ssem).start()
            return c
        lax.fori_loop(0, tm, start, 0, unroll=8)

        def wait(r, c):
            _row_copy(acc_hbm, prv, off_prv + idxp_ref[0, 0, r], r, ssem).wait()
            return c
        lax.fori_loop(0, tm, wait, 0, unroll=8)

    for r in range(tm):
        _row_copy(acc_hbm, cur, off_cur + idx_ref[0, 0, r], r, gsem).start()
    for r in range(tm):
        _row_copy_back(prv, acc_hbm, off_prv + idxp_ref[0, 0, r], r, ssem).start()
    ye = jnp.dot(h_ref[...], wd_ref[...], preferred_element_type=F32) * g_ref[:, 0:1]
    for r in range(tm):
        _row_copy(acc_hbm, cur, off_cur + idx_ref[0, 0, r], r, gsem).wait()
    for r in range(tm):
        _row_copy_back(prv, acc_hbm, off_prv + idxp_ref[0, 0, r], r, ssem).wait()
    cur[...] = cur[...] + ye

    @pl.when(s == n_steps - 1)
    def _():
        def start(r, c):
            _row_copy_back(cur, acc_hbm, off_cur + idx_ref[0, 0, r], r, ssem).start()
            return c
        lax.fori_loop(0, tm, start, 0, unroll=8)

        def wait(r, c):
            _row_copy_back(cur, acc_hbm, off_cur + idx_ref[0, 0, r], r, ssem).wait()
            return c
        lax.fori_loop(0, tm, wait, 0, unroll=8)


def _moe_down(idx3, h, w_down, g, acc, cap, tm):
    rows, ff = h.shape
    n_e, _, d = w_down.shape
    per_e = cap // tm
    n_steps = n_e * per_e
    n_tok = acc.shape[0] // 2
    blk = lambda e, i: e * per_e + i
    return pl.pallas_call(
        functools.partial(_moe_down_kernel, per_e, n_tok),
        grid=(n_e, per_e),
        in_specs=[pl.BlockSpec((1, 1, tm), lambda e, i: (blk(e, i), 0, 0), memory_space=pltpu.SMEM),
                  pl.BlockSpec((1, 1, tm), lambda e, i: ((blk(e, i) + n_steps - 1) % n_steps, 0, 0),
                               memory_space=pltpu.SMEM),
                  pl.BlockSpec((tm, ff), lambda e, i: (blk(e, i), 0)),
                  pl.BlockSpec((None, ff, d), lambda e, i: (e, 0, 0), pipeline_mode=pl.Buffered(1)),
                  pl.BlockSpec((tm, LANES), lambda e, i: (blk(e, i), 0)),
                  pl.BlockSpec(memory_space=pl.ANY)],
        out_specs=pl.BlockSpec(memory_space=pl.ANY),
        out_shape=jax.ShapeDtypeStruct(acc.shape, acc.dtype),
        scratch_shapes=[pltpu.VMEM((2, tm, d), F32), pltpu.SemaphoreType.DMA(()), pltpu.SemaphoreType.DMA(())],
        input_output_aliases={5: 0},
        compiler_params=_cparams(("arbitrary", "arbitrary")),
        name="moe_down",
    )(idx3, idx3, h, w_down, g, acc)


def _ln2_kernel(x_ref, g_ref, b_ref, y_ref, ybf_ref):
    y = _layer_norm_rows(x_ref[0] + x_ref[1], g_ref[...], b_ref[...])
    y_ref[...] = y
    ybf_ref[...] = y.astype(ybf_ref.dtype)


def _ln2(x, g, b, tm):
    _, m, d = x.shape
    row = lambda i: (i, 0)
    return pl.pallas_call(
        _ln2_kernel,
        grid=(m // tm,),
        in_specs=[pl.BlockSpec((2, tm, d), lambda i: (0, i, 0)),
                  pl.BlockSpec((1, d), lambda i: (0, 0)),
                  pl.BlockSpec((1, d), lambda i: (0, 0))],
        out_specs=[pl.BlockSpec((tm, d), row), pl.BlockSpec((tm, d), row)],
        out_shape=[jax.ShapeDtypeStruct((m, d), F32), jax.ShapeDtypeStruct((m, d), BF16)],
        compiler_params=_cparams(("parallel",)),
        name="ln2",
    )(x, g, b)


def _rope_tables(seq):
    rows = seq // GRID_W
    row = jnp.repeat(jnp.arange(rows), GRID_W).astype(F32)
    col = (jnp.arange(rows * GRID_W) % GRID_W).astype(F32)
    inv = ROPE_THETA ** (-jnp.arange(ROPE_FREQS, dtype=F32) / ROPE_FREQS)
    ang_r = row[:, None] * inv
    ang_c = col[:, None] * inv
    cos_t = jnp.concatenate([jnp.cos(ang_r), jnp.cos(ang_r), jnp.cos(ang_c), jnp.cos(ang_c)], axis=1)
    sin_t = jnp.concatenate([-jnp.sin(ang_r), jnp.sin(ang_r), -jnp.sin(ang_c), jnp.sin(ang_c)], axis=1)
    return cos_t, sin_t


def _prep_layer(l, w_in, b_gate, q_norm_g, k_norm_g, sgu_ln_g, sgu_ln_b, w_s, b_s, w_mem_kv, w_branch,
                w_o, ln1_g, ln1_b, w_router, w_gate, w_up, w_down, ln2_g, ln2_b):
    d = w_in.shape[1]
    wi = w_in[l]
    c_q, c_k, c_v = ATTN_WIDTH, ATTN_WIDTH + KV_WIDTH, ATTN_WIDTH + 2 * KV_WIDTH
    c_u = c_v + SGU_WIDTH
    c_vb = c_u + SGU_WIDTH
    c_qm = c_vb + MEM_WIDTH
    wr = jnp.pad(w_router[l], ((0, 0), (0, LANES - N_EXPERTS)))
    wr_hi = wr.astype(BF16)
    wr_lo = (wr - wr_hi.astype(F32)).astype(BF16)
    return dict(
        w_attn=jnp.concatenate([wi[:, :c_v], wi[:, c_vb:c_qm]], axis=1).astype(BF16),
        w_uv=wi[:, c_v:c_vb].astype(BF16),
        w_g=wi[:, c_qm:].astype(BF16),
        b_g=b_gate[l].reshape(1, N_BRANCH * d),
        gq=q_norm_g[l].reshape(1, HEAD_DIM), gk=k_norm_g[l].reshape(1, HEAD_DIM),
        sgu_g=sgu_ln_g[l].reshape(1, SGU_WIDTH), sgu_b=sgu_ln_b[l].reshape(1, SGU_WIDTH),
        w_s=w_s[l].astype(BF16), b_s=b_s[l].reshape(SGU_GROUPS, CHUNK, 1),
        w_mem_kv=w_mem_kv[l].astype(BF16), w_branch=w_branch[l].astype(BF16), w_o=w_o[l].astype(BF16),
        ln1_g=ln1_g[l].reshape(1, d), ln1_b=ln1_b[l].reshape(1, d),
        wr=jnp.concatenate([wr_hi, wr_lo], axis=1),
        layer=l, w_gate=w_gate, w_up=w_up, w_down=w_down,
        ln2_g=ln2_g[l].reshape(1, d), ln2_b=ln2_b[l].reshape(1, d),
    )


def _run_trunk(x, mem, layers, alpha, wd_bf16):
    bsz, seq, d = x.shape
    n_mem = mem.shape[1]
    m = bsz * seq
    cap = EC_FACTOR * m // N_EXPERTS
    cos_t, sin_t = _rope_tables(seq)
    xf = x.reshape(m, d)
    x_bf = xf.astype(BF16)
    mem_bf = mem.reshape(bsz * n_mem, d).astype(BF16)
    tm_moe = min(512, cap)
    for p in layers:
        q, k, v, qm = _proj_attn(x_bf, p["w_attn"], cos_t, sin_t, p["gq"], p["gk"], seq, tm=512)
        out_b = _proj_sgu(x_bf, p["w_uv"], p["sgu_g"], p["sgu_b"], p["w_s"], p["b_s"], tm=512)
        out_a = _gqa(q, k, v, bsz, seq, tq=256)
        kv = _mm_plain(mem_bf, p["w_mem_kv"], tm=min(512, bsz * n_mem))
        out_c = _mem_attn(qm, kv, bsz, seq, n_mem, tq=512)
        merged = _branch_merge(x_bf, p["w_g"], p["b_g"], out_a, out_b, out_c, p["w_branch"], tm=512, tn=512)
        x1e, acc, aff_t = _out_ln1(merged, p["w_o"], xf, p["ln1_g"], p["ln1_b"], p["wr"], alpha, tm=512)
        idx = _route(aff_t.reshape(N_EXPERTS, m // LANES, LANES), cap)
        idx3 = idx.reshape(N_EXPERTS * cap // tm_moe, 1, tm_moe)
        xe, g = _moe_gather(idx3, x1e, cap, tm_moe)
        cast_wd = p["layer"] not in wd_bf16
        res = _moe_up(xe, p["w_gate"], p["w_up"], p["w_down"] if cast_wd else None, p["layer"], cap,
                      tm=min(1024, cap), tf=min(512, p["w_gate"].shape[3]))
        if cast_wd:
            h, wd_bf16[p["layer"]] = res
        else:
            h = res
        acc = _moe_down(idx3, h, wd_bf16[p["layer"]], g, acc.reshape(2 * m, d), cap, tm_moe)
        xf, x_bf = _ln2(acc.reshape(2, m, d), p["ln2_g"], p["ln2_b"], tm=512)
    return xf.reshape(bsz, seq, d)


def kernel(x_prompt, x_sample, mem_prompt, mem_sample, w_in, b_gate, q_norm_g, k_norm_g, sgu_ln_g, sgu_ln_b,
           w_s, b_s, w_mem_kv, w_branch, w_o, ln1_g, ln1_b, w_router, w_gate, w_up, w_down, ln2_g, ln2_b):
    depth = w_in.shape[0]
    alpha = (2 * depth) ** 0.25
    layers = [_prep_layer(l, w_in, b_gate, q_norm_g, k_norm_g, sgu_ln_g, sgu_ln_b, w_s, b_s, w_mem_kv,
                          w_branch, w_o, ln1_g, ln1_b, w_router, w_gate, w_up, w_down, ln2_g, ln2_b)
              for l in range(depth)]
    wd_bf16 = {}
    y_prompt = _run_trunk(x_prompt, mem_prompt, layers, alpha, wd_bf16)
    y_sample = _run_trunk(x_sample, mem_sample, layers, alpha, wd_bf16)
    return (y_prompt, y_sample)
```

```python
import functools

import jax
import jax.numpy as jnp
from jax import lax
from jax.experimental import pallas as pl
from jax.experimental.pallas import tpu as pltpu

F32 = jnp.float32
BF16 = jnp.bfloat16

HEAD_DIM = 128
N_Q_HEADS = 8
N_KV_HEADS = 2
Q_PER_KV = N_Q_HEADS // N_KV_HEADS
ATTN_WIDTH = N_Q_HEADS * HEAD_DIM
KV_WIDTH = N_KV_HEADS * HEAD_DIM
SGU_GROUPS = 8
SGU_WIDTH = SGU_GROUPS * 128
CHUNK = 128
MEM_HEADS = 4
MEM_WIDTH = MEM_HEADS * HEAD_DIM
N_BRANCH = 3
N_EXPERTS = 16
EC_FACTOR = 2
GRID_W = 64
ROPE_THETA = 10000.0
ROPE_FREQS = HEAD_DIM // 4
EPS = 1e-6
LANES = 128
ROW_SUB = 256
VMEM_LIMIT = 56 * 1024 * 1024


def _cparams(sem, vmem=VMEM_LIMIT):
    return pltpu.CompilerParams(dimension_semantics=sem, vmem_limit_bytes=vmem)


def _gelu_tanh(x):
    return 0.5 * x * (1.0 + jnp.tanh(0.7978845608028654 * (x + 0.044715 * (x * x * x))))


def _sigmoid(x):
    return 1.0 / (1.0 + jnp.exp(-x))


def _layer_norm_rows(x, g, b):
    mu = jnp.mean(x, axis=-1, keepdims=True)
    xc = x - mu
    var = jnp.mean(xc * xc, axis=-1, keepdims=True)
    return xc * lax.rsqrt(var + EPS) * g + b


def _mm_plain_kernel(x_ref, w_ref, o_ref):
    o_ref[...] = jnp.dot(x_ref[...], w_ref[...], preferred_element_type=F32).astype(o_ref.dtype)


def _mm_plain(x, w, tm):
    m, k = x.shape
    n = w.shape[1]
    return pl.pallas_call(
        _mm_plain_kernel,
        grid=(m // tm,),
        in_specs=[pl.BlockSpec((tm, k), lambda i: (i, 0)),
                  pl.BlockSpec((k, n), lambda i: (0, 0))],
        out_specs=pl.BlockSpec((tm, n), lambda i: (i, 0)),
        out_shape=jax.ShapeDtypeStruct((m, n), BF16),
        compiler_params=_cparams(("parallel",)),
        name="mm_plain",
    )(x, w)


def _proj_attn_kernel(x_ref, w_ref, cos_ref, sin_ref, gq_ref, gk_ref, q_ref, k_ref, v_ref, qm_ref):
    scale = HEAD_DIM ** -0.5
    gq = gq_ref[...]
    gk = gk_ref[...]
    tm = x_ref.shape[0]
    for r0 in range(0, tm, ROW_SUB):
        rows = pl.ds(r0, ROW_SUB)
        acc = jnp.dot(x_ref[rows, :], w_ref[...], preferred_element_type=F32)
        cos = cos_ref[rows, :]
        sin = sin_ref[rows, :]
        lane = lax.broadcasted_iota(jnp.int32, cos.shape, 1)
        first_half = (lane % (2 * ROPE_FREQS)) < ROPE_FREQS

        def norm_rope(blk, g):
            ms = jnp.mean(blk * blk, axis=-1, keepdims=True)
            y = blk * lax.rsqrt(ms + EPS) * g
            partner = jnp.where(first_half,
                                pltpu.roll(y, HEAD_DIM - ROPE_FREQS, 1),
                                pltpu.roll(y, ROPE_FREQS, 1))
            return y * cos + partner * sin

        for h in range(N_Q_HEADS):
            blk = acc[:, h * HEAD_DIM:(h + 1) * HEAD_DIM]
            q_ref[rows, h * HEAD_DIM:(h + 1) * HEAD_DIM] = (norm_rope(blk, gq) * scale).astype(q_ref.dtype)
        for h in range(N_KV_HEADS):
            c0 = ATTN_WIDTH + h * HEAD_DIM
            k_ref[rows, h * HEAD_DIM:(h + 1) * HEAD_DIM] = norm_rope(acc[:, c0:c0 + HEAD_DIM], gk).astype(k_ref.dtype)
        c0 = ATTN_WIDTH + KV_WIDTH
        v_ref[rows, :] = acc[:, c0:c0 + KV_WIDTH].astype(v_ref.dtype)
        c0 += KV_WIDTH
        qm_ref[rows, :] = (acc[:, c0:c0 + MEM_WIDTH] * scale).astype(qm_ref.dtype)


def _proj_attn(x_bf, w_attn, cos_t, sin_t, gq, gk, seq, tm):
    m, d = x_bf.shape
    n = w_attn.shape[1]
    per_seq = seq // tm
    row = lambda i: (i, 0)
    const = lambda i: (0, 0)
    return pl.pallas_call(
        _proj_attn_kernel,
        grid=(m // tm,),
        in_specs=[pl.BlockSpec((tm, d), row),
                  pl.BlockSpec((d, n), const),
                  pl.BlockSpec((tm, HEAD_DIM), lambda i: (i % per_seq, 0)),
                  pl.BlockSpec((tm, HEAD_DIM), lambda i: (i % per_seq, 0)),
                  pl.BlockSpec((1, HEAD_DIM), const),
                  pl.BlockSpec((1, HEAD_DIM), const)],
        out_specs=[pl.BlockSpec((tm, ATTN_WIDTH), row),
                   pl.BlockSpec((tm, KV_WIDTH), row),
                   pl.BlockSpec((tm, KV_WIDTH), row),
                   pl.BlockSpec((tm, MEM_WIDTH), row)],
        out_shape=[jax.ShapeDtypeStruct((m, ATTN_WIDTH), BF16),
                   jax.ShapeDtypeStruct((m, KV_WIDTH), BF16),
                   jax.ShapeDtypeStruct((m, KV_WIDTH), BF16),
                   jax.ShapeDtypeStruct((m, MEM_WIDTH), BF16)],
        compiler_params=_cparams(("parallel",)),
        name="proj_attn",
    )(x_bf, w_attn, cos_t, sin_t, gq, gk)


def _proj_sgu_kernel(x_ref, w_ref, lng_ref, lnb_ref, ws_ref, bs_ref, o_ref):
    acc = jnp.dot(x_ref[...], w_ref[...], preferred_element_type=F32)
    u = _gelu_tanh(acc[:, :SGU_WIDTH])
    v = _gelu_tanh(acc[:, SGU_WIDTH:])
    vn = _layer_norm_rows(v, lng_ref[...], lnb_ref[...]).astype(BF16)
    tm = acc.shape[0]
    for c in range(tm // CHUNK):
        r0 = c * CHUNK
        for g in range(SGU_GROUPS):
            c0 = g * 128
            mixed = jnp.dot(ws_ref[g], vn[r0:r0 + CHUNK, c0:c0 + 128], preferred_element_type=F32)
            mixed = mixed + bs_ref[g]
            o_ref[r0:r0 + CHUNK, c0:c0 + 128] = (u[r0:r0 + CHUNK, c0:c0 + 128] * mixed).astype(o_ref.dtype)


def _proj_sgu(x_bf, w_uv, ln_g, ln_b, w_s, b_s, tm):
    m, d = x_bf.shape
    return pl.pallas_call(
        _proj_sgu_kernel,
        grid=(m // tm,),
        in_specs=[pl.BlockSpec((tm, d), lambda i: (i, 0)),
                  pl.BlockSpec((d, 2 * SGU_WIDTH), lambda i: (0, 0)),
                  pl.BlockSpec((1, SGU_WIDTH), lambda i: (0, 0)),
                  pl.BlockSpec((1, SGU_WIDTH), lambda i: (0, 0)),
                  pl.BlockSpec((SGU_GROUPS, CHUNK, CHUNK), lambda i: (0, 0, 0)),
                  pl.BlockSpec((SGU_GROUPS, CHUNK, 1), lambda i: (0, 0, 0))],
        out_specs=pl.BlockSpec((tm, SGU_WIDTH), lambda i: (i, 0)),
        out_shape=jax.ShapeDtypeStruct((m, SGU_WIDTH), BF16),
        compiler_params=_cparams(("parallel",)),
        name="proj_sgu",
    )(x_bf, w_uv, ln_g, ln_b, w_s, b_s)


def _softmax_pv(s, v):
    mx = jnp.max(s, axis=-1, keepdims=True)
    p = jnp.exp(s - mx)
    l = jnp.sum(p, axis=-1, keepdims=True)
    o = jnp.dot(p.astype(BF16), v, preferred_element_type=F32)
    return o / l


_NT = (((1,), (1,)), ((), ()))


def _gqa_kernel(q_ref, k_ref, v_ref, o_ref):
    k = k_ref[...]
    v = v_ref[...]
    for h in range(Q_PER_KV):
        q = q_ref[:, h * HEAD_DIM:(h + 1) * HEAD_DIM]
        s = lax.dot_general(q, k, _NT, preferred_element_type=F32)
        o_ref[:, h * HEAD_DIM:(h + 1) * HEAD_DIM] = _softmax_pv(s, v).astype(o_ref.dtype)


def _gqa(q, k, v, bsz, seq, tq):
    m = q.shape[0]
    nq = seq // tq
    gw = Q_PER_KV * HEAD_DIM
    return pl.pallas_call(
        _gqa_kernel,
        grid=(bsz, N_KV_HEADS, nq),
        in_specs=[pl.BlockSpec((tq, gw), lambda b, g, i: (b * nq + i, g)),
                  pl.BlockSpec((seq, HEAD_DIM), lambda b, g, i: (b, g)),
                  pl.BlockSpec((seq, HEAD_DIM), lambda b, g, i: (b, g))],
        out_specs=pl.BlockSpec((tq, gw), lambda b, g, i: (b * nq + i, g)),
        out_shape=jax.ShapeDtypeStruct((m, ATTN_WIDTH), BF16),
        compiler_params=_cparams(("parallel", "parallel", "parallel")),
        name="gqa",
    )(q, k, v)


def _mem_attn_kernel(q_ref, kv_ref, o_ref):
    for h in range(MEM_HEADS):
        q = q_ref[:, h * HEAD_DIM:(h + 1) * HEAD_DIM]
        k = kv_ref[:, h * HEAD_DIM:(h + 1) * HEAD_DIM]
        v = kv_ref[:, MEM_WIDTH + h * HEAD_DIM:MEM_WIDTH + (h + 1) * HEAD_DIM]
        s = lax.dot_general(q, k, _NT, preferred_element_type=F32)
        o_ref[:, h * HEAD_DIM:(h + 1) * HEAD_DIM] = _softmax_pv(s, v).astype(o_ref.dtype)


def _mem_attn(qm, kv, bsz, seq, n_mem, tq):
    m = qm.shape[0]
    nq = seq // tq
    return pl.pallas_call(
        _mem_attn_kernel,
        grid=(bsz, nq),
        in_specs=[pl.BlockSpec((tq, MEM_WIDTH), lambda b, i: (b * nq + i, 0)),
                  pl.BlockSpec((n_mem, 2 * MEM_WIDTH), lambda b, i: (b, 0))],
        out_specs=pl.BlockSpec((tq, MEM_WIDTH), lambda b, i: (b * nq + i, 0)),
        out_shape=jax.ShapeDtypeStruct((m, MEM_WIDTH), BF16),
        compiler_params=_cparams(("parallel", "parallel")),
        name="mem_attn",
    )(qm, kv)


def _branch_merge_kernel(x_ref, wg0, wg1, wg2, bg0, bg1, bg2, oa_ref, ob_ref, oc_ref,
                         wba, wbb, wbc, o_ref):
    x = x_ref[...]
    acc = None
    for wg, bg, o, wb in ((wg0, bg0, oa_ref, wba), (wg1, bg1, ob_ref, wbb), (wg2, bg2, oc_ref, wbc)):
        gate = _sigmoid(jnp.dot(x, wg[...], preferred_element_type=F32) + bg[...])
        p = jnp.dot(o[...], wb[...], preferred_element_type=F32)
        acc = gate * p if acc is None else acc + gate * p
    o_ref[...] = acc.astype(o_ref.dtype)


def _branch_merge(x_bf, w_g, b_g, oa, ob, oc, w_branch, tm, tn):
    m, d = x_bf.shape
    nj = d // tn
    row = lambda j, i: (i, 0)
    in_specs = [pl.BlockSpec((tm, d), row)]
    in_specs += [pl.BlockSpec((d, tn), functools.partial(lambda j, i, br: (0, br * nj + j), br=br))
                 for br in range(N_BRANCH)]
    in_specs += [pl.BlockSpec((1, tn), functools.partial(lambda j, i, br: (0, br * nj + j), br=br))
                 for br in range(N_BRANCH)]
    in_specs += [pl.BlockSpec((tm, ATTN_WIDTH), row),
                 pl.BlockSpec((tm, SGU_WIDTH), row),
                 pl.BlockSpec((tm, MEM_WIDTH), row),
                 pl.BlockSpec((ATTN_WIDTH, tn), lambda j, i: (0, j)),
                 pl.BlockSpec((SGU_WIDTH, tn), lambda j, i: (ATTN_WIDTH // SGU_WIDTH, j)),
                 pl.BlockSpec((MEM_WIDTH, tn), lambda j, i: ((ATTN_WIDTH + SGU_WIDTH) // MEM_WIDTH, j))]
    return pl.pallas_call(
        _branch_merge_kernel,
        grid=(nj, m // tm),
        in_specs=in_specs,
        out_specs=pl.BlockSpec((tm, tn), lambda j, i: (i, j)),
        out_shape=jax.ShapeDtypeStruct((m, d), BF16),
        compiler_params=_cparams(("parallel", "parallel")),
        name="branch_merge",
    )(x_bf, w_g, w_g, w_g, b_g, b_g, b_g, oa, ob, oc, w_branch, w_branch, w_branch)


def _split_bf16(x):
    hi = x.astype(BF16)
    lo = (x - hi.astype(F32)).astype(BF16)
    return hi, lo


def _out_ln1_kernel(alpha, mg_ref, wo_ref, x_ref, g_ref, b_ref, wr_ref,
                    x1e_ref, xs_ref, afft_ref):
    tm, d = x_ref.shape
    sub = min(ROW_SUB, tm)
    for r0 in range(0, tm, sub):
        rows = pl.ds(r0, sub)
        y = jnp.dot(mg_ref[rows, :], wo_ref[...], preferred_element_type=F32)
        x1 = _layer_norm_rows(alpha * x_ref[rows, :] + y, g_ref[...], b_ref[...])
        x1e_ref[rows, :d] = x1
        xs_ref[0, rows, :] = alpha * x1
        xs_ref[1, rows, :] = jnp.zeros_like(x1)
        xh, xl = _split_bf16(x1)
        wrh = wr_ref[:, :LANES]
        logits = (jnp.dot(xh, wrh, preferred_element_type=F32)
                  + jnp.dot(xh, wr_ref[:, LANES:], preferred_element_type=F32)
                  + jnp.dot(xl, wrh, preferred_element_type=F32))
        lane = lax.broadcasted_iota(jnp.int32, logits.shape, 1)
        valid = lane < N_EXPERTS
        logits = jnp.where(valid, logits, -1e30)
        mx = jnp.max(logits, axis=-1, keepdims=True)
        e = jnp.where(valid, jnp.exp(logits - mx), 0.0)
        aff = e / jnp.sum(e, axis=-1, keepdims=True)
        x1e_ref[rows, d:] = aff
        afft_ref[:, rows] = aff.T[:N_EXPERTS, :]


def _out_ln1(merged, w_o, x, ln_g, ln_b, wr, alpha, tm):
    m, d = x.shape
    row = lambda i: (i, 0)
    const = lambda i: (0, 0)
    return pl.pallas_call(
        functools.partial(_out_ln1_kernel, alpha),
        grid=(m // tm,),
        in_specs=[pl.BlockSpec((tm, d), row),
                  pl.BlockSpec((d, d), const, pipeline_mode=pl.Buffered(1)),
                  pl.BlockSpec((tm, d), row),
                  pl.BlockSpec((1, d), const),
                  pl.BlockSpec((1, d), const),
                  pl.BlockSpec((d, 2 * LANES), const, pipeline_mode=pl.Buffered(1))],
        out_specs=[pl.BlockSpec((tm, d + LANES), row),
                   pl.BlockSpec((2, tm, d), lambda i: (0, i, 0)),
                   pl.BlockSpec((N_EXPERTS, tm), lambda i: (0, i))],
        out_shape=[jax.ShapeDtypeStruct((m, d + LANES), F32),
                   jax.ShapeDtypeStruct((2, m, d), F32),
                   jax.ShapeDtypeStruct((N_EXPERTS, m), F32)],
        compiler_params=_cparams(("parallel",)),
        name="out_ln1",
    )(merged, w_o, x, ln_g, ln_b, wr)


_BISECT_STEPS = 40


def _route_kernel(cap, a_ref, idx_ref):
    a = a_ref[...]
    n_e, n_r, _ = a.shape
    capf = float(cap)

    def count(mask):
        part = jnp.sum(jnp.where(mask, 1.0, 0.0), axis=2, keepdims=True)
        return jnp.sum(part, axis=1, keepdims=True)

    def bisect(_, carry):
        lo, hi = carry
        mid = 0.5 * (lo + hi)
        ok = count(a >= mid) >= capf
        return jnp.where(ok, mid, lo), jnp.where(ok, hi, mid)

    lo0 = jnp.zeros((n_e, 1, 1), F32)
    hi0 = jnp.full((n_e, 1, 1), 2.0, F32)
    _, hi = lax.fori_loop(0, _BISECT_STEPS, bisect, (lo0, hi0))

    def below_max(hi):
        part = jnp.max(jnp.where(a < hi, a, -1.0), axis=2, keepdims=True)
        return jnp.max(part, axis=1, keepdims=True)

    def refine_cond(carry):
        hi, t = carry
        short = jnp.where(count(a >= t) < capf, 1.0, 0.0)
        return jnp.max(short) > 0.0

    def refine_body(carry):
        hi, t = carry
        hi = jnp.where(count(a >= t) < capf, t, hi)
        return hi, below_max(hi)

    _, thr = lax.while_loop(refine_cond, refine_body, (hi, below_max(hi)))

    ri = lax.broadcasted_iota(jnp.int32, (LANES, LANES), 0)
    ci = lax.broadcasted_iota(jnp.int32, (LANES, LANES), 1)
    tri = jnp.where(ri <= ci, 1.0, 0.0).astype(BF16)
    rr = lax.broadcasted_iota(jnp.int32, (n_r, n_r), 0)
    rc = lax.broadcasted_iota(jnp.int32, (n_r, n_r), 1)
    lstrict = jnp.where(rc < rr, 1.0, 0.0).astype(BF16)

    def prefix_incl(mask_f):
        within = jnp.dot(mask_f.reshape(n_e * n_r, LANES).astype(BF16), tri,
                         preferred_element_type=F32).reshape(n_e, n_r, LANES)
        outs = []
        for e in range(n_e):
            off = jnp.dot(lstrict, within[e].astype(BF16), preferred_element_type=F32)
            outs.append(within[e] + off[:, LANES - 1:LANES])
        return outs

    gt = a > thr
    eq = a == thr
    need = capf - count(gt)
    eq_f = jnp.where(eq, 1.0, 0.0)
    eq_incl = prefix_incl(eq_f)
    j_row = lax.broadcasted_iota(jnp.int32, (1, cap), 1).astype(F32)
    r_col = lax.broadcasted_iota(jnp.int32, (n_r, 1), 0).astype(F32)
    sel_list = []
    for e in range(n_e):
        eq_excl = eq_incl[e] - eq_f[e]
        sel_list.append(jnp.where(gt[e] | (eq[e] & (eq_excl < need[e])), 1.0, 0.0))
    slot_incl = prefix_incl(jnp.stack(sel_list, axis=0))
    for e in range(n_e):
        s_e = slot_incl[e]
        c_e = s_e[:, LANES - 1:LANES]
        blk = jnp.sum(jnp.where(c_e <= j_row, 1.0, 0.0), axis=0, keepdims=True)
        onehot_t = jnp.where(r_col == blk, 1.0, 0.0).astype(BF16)
        s_hi = jnp.floor(s_e * (1.0 / 64.0))
        s_lo = s_e - 64.0 * s_hi
        rows_t = (64.0 * jnp.dot(s_hi.T.astype(BF16), onehot_t, preferred_element_type=F32)
                  + jnp.dot(s_lo.T.astype(BF16), onehot_t, preferred_element_type=F32))
        within = jnp.sum(jnp.where(rows_t <= j_row, 1.0, 0.0), axis=0, keepdims=True)
        idx_ref[e:e + 1, :] = (blk * float(LANES) + within).astype(jnp.int32)


def _route(aff3, cap):
    n_e, n_r, _ = aff3.shape
    return pl.pallas_call(
        functools.partial(_route_kernel, cap),
        grid=(1,),
        in_specs=[pl.BlockSpec((n_e, n_r, LANES), lambda i: (0, 0, 0))],
        out_specs=pl.BlockSpec((n_e, cap), lambda i: (0, 0)),
        out_shape=jax.ShapeDtypeStruct((n_e, cap), jnp.int32),
        compiler_params=_cparams(("arbitrary",)),
        name="route",
    )(aff3)


def _row_copy(src_hbm, dst_vmem, tok, r, sem):
    return pltpu.make_async_copy(src_hbm.at[pl.ds(tok, 1), :], dst_vmem.at[pl.ds(r, 1), :], sem)


def _row_copy_back(src_vmem, dst_hbm, tok, r, sem):
    return pltpu.make_async_copy(src_vmem.at[pl.ds(r, 1), :], dst_hbm.at[pl.ds(tok, 1), :], sem)


def _gather_start(idx_ref, src_hbm, buf, sem, n_rows):
    def body(r, c):
        _row_copy(src_hbm, buf, idx_ref[0, 0, r], r, sem).start()
        return c
    lax.fori_loop(0, n_rows, body, 0, unroll=8)


def _gather_wait(idx_ref, src_hbm, buf, sem, n_rows):
    def body(r, c):
        _row_copy(src_hbm, buf, idx_ref[0, 0, r], r, sem).wait()
        return c
    lax.fori_loop(0, n_rows, body, 0, unroll=8)


def _scatter_start(idx_ref, buf, dst_hbm, sem, n_rows):
    def body(r, c):
        _row_copy_back(buf, dst_hbm, idx_ref[0, 0, r], r, sem).start()
        return c
    lax.fori_loop(0, n_rows, body, 0, unroll=8)


def _scatter_wait(idx_ref, buf, dst_hbm, sem, n_rows):
    def body(r, c):
        _row_copy_back(buf, dst_hbm, idx_ref[0, 0, r], r, sem).wait()
        return c
    lax.fori_loop(0, n_rows, body, 0, unroll=8)


def _moe_gather_kernel(tiles_per_expert, d, idx_ref, idxn_ref, x1e_hbm, xe_ref, g_ref, bufs, sems):
    tm = bufs.shape[1]
    s = pl.program_id(0)
    last = pl.num_programs(0) - 1
    slot = s % 2
    cur = bufs.at[slot]
    nxt = bufs.at[1 - slot]

    @pl.when(s == 0)
    def _():
        _gather_start(idx_ref, x1e_hbm, cur, sems.at[slot], tm)

    @pl.when(s < last)
    def _():
        for r in range(tm):
            _row_copy(x1e_hbm, nxt, idxn_ref[0, 0, r], r, sems.at[1 - slot]).start()

    for r in range(tm):
        _row_copy(x1e_hbm, cur, idx_ref[0, 0, r], r, sems.at[slot]).wait()
    e = s // tiles_per_expert
    xe_ref[...] = cur[:, :d].astype(xe_ref.dtype)
    aff = cur[:, d:]
    lane = lax.broadcasted_iota(jnp.int32, aff.shape, 1)
    g = jnp.sum(jnp.where(lane == e, aff, 0.0), axis=-1, keepdims=True)
    g_ref[...] = jnp.broadcast_to(g, g_ref.shape)


def _moe_gather(idx3, x1e, cap, tm):
    n_tiles = idx3.shape[0]
    d = x1e.shape[1] - LANES
    rows = n_tiles * tm
    return pl.pallas_call(
        functools.partial(_moe_gather_kernel, cap // tm, d),
        grid=(n_tiles,),
        in_specs=[pl.BlockSpec((1, 1, tm), lambda i: (i, 0, 0), memory_space=pltpu.SMEM),
                  pl.BlockSpec((1, 1, tm), lambda i: (jnp.minimum(i + 1, n_tiles - 1), 0, 0),
                               memory_space=pltpu.SMEM),
                  pl.BlockSpec(memory_space=pl.ANY)],
        out_specs=[pl.BlockSpec((tm, d), lambda i: (i, 0)),
                   pl.BlockSpec((tm, LANES), lambda i: (i, 0))],
        out_shape=[jax.ShapeDtypeStruct((rows, d), BF16),
                   jax.ShapeDtypeStruct((rows, LANES), F32)],
        scratch_shapes=[pltpu.VMEM((2, tm, d + LANES), F32), pltpu.SemaphoreType.DMA((2,))],
        compiler_params=_cparams(("arbitrary",)),
        name="moe_gather",
    )(idx3, idx3, x1e)


def _moe_up_kernel(cast_wd, layer, x_ref, wg_hbm, wu_hbm, *rest):
    if cast_wd:
        wd_hbm, h_ref, wd_out_ref, wbuf, wdbuf, sems = rest
    else:
        h_ref, wbuf, sems = rest
    nf = pl.num_programs(1)
    n_pairs = pl.num_programs(0) * nf
    t = pl.program_id(0) * nf + pl.program_id(1)
    slot = t % 2
    tf = wbuf.shape[-1]

    def tile_copies(pair, sl):
        ee = pair // nf
        c0 = pl.multiple_of((pair % nf) * tf, tf)
        cps = [pltpu.make_async_copy(wg_hbm.at[layer, ee, :, pl.ds(c0, tf)], wbuf.at[sl, 0], sems.at[sl, 0]),
               pltpu.make_async_copy(wu_hbm.at[layer, ee, :, pl.ds(c0, tf)], wbuf.at[sl, 1], sems.at[sl, 1])]
        if cast_wd:
            cps.append(pltpu.make_async_copy(wd_hbm.at[layer, ee, pl.ds(c0, tf), :], wdbuf.at[sl],
                                             sems.at[sl, 2]))
        return cps

    @pl.when(pl.program_id(2) == 0)
    def _():
        @pl.when(t == 0)
        def _():
            for c in tile_copies(t, slot):
                c.start()

        @pl.when(t + 1 < n_pairs)
        def _():
            for c in tile_copies(t + 1, 1 - slot):
                c.start()

        for c in tile_copies(t, slot):
            c.wait()
        if cast_wd:
            wd_out_ref[...] = wdbuf[slot].astype(wd_out_ref.dtype)

    x = x_ref[...]
    a = jnp.dot(x, wbuf[slot, 0].astype(BF16), preferred_element_type=F32)
    b = jnp.dot(x, wbuf[slot, 1].astype(BF16), preferred_element_type=F32)
    h_ref[...] = (a * _sigmoid(a) * b).astype(h_ref.dtype)


def _moe_up(xe, w_gate, w_up, w_down, layer, cap, tm, tf):
    rows, d = xe.shape
    _, n_e, _, ff = w_gate.shape
    per_e = cap // tm
    cast_wd = w_down is not None
    hbm = pl.BlockSpec(memory_space=pl.ANY)
    in_specs = [pl.BlockSpec((tm, d), lambda e, f, i: (e * per_e + i, 0)), hbm, hbm]
    out_specs = [pl.BlockSpec((tm, tf), lambda e, f, i: (e * per_e + i, f))]
    out_shape = [jax.ShapeDtypeStruct((rows, ff), BF16)]
    scratch = [pltpu.VMEM((2, 2, d, tf), F32)]
    args = [xe, w_gate, w_up]
    if cast_wd:
        in_specs.append(hbm)
        out_specs.append(pl.BlockSpec((None, tf, d), lambda e, f, i: (e, f, 0)))
        out_shape.append(jax.ShapeDtypeStruct((n_e, ff, d), BF16))
        scratch.append(pltpu.VMEM((2, tf, d), F32))
        args.append(w_down)
    scratch.append(pltpu.SemaphoreType.DMA((2, 3 if cast_wd else 2)))
    res = pl.pallas_call(
        functools.partial(_moe_up_kernel, cast_wd, layer),
        grid=(n_e, ff // tf, per_e),
        in_specs=in_specs,
        out_specs=out_specs,
        out_shape=out_shape,
        scratch_shapes=scratch,
        compiler_params=_cparams(("arbitrary", "arbitrary", "arbitrary")),
        name="moe_up_cast" if cast_wd else "moe_up",
    )(*args)
    return res if cast_wd else res[0]


def _moe_down_kernel(per_e, n_tok, idx_ref, idxp_ref, h_ref, wd_hbm, g_ref, acc_in_hbm, acc_hbm,
                     bufs, gsem, ssem, wdbuf, wsems):
    del acc_in_hbm
    tm = bufs.shape[1]
    e = pl.program_id(0)
    s = e * per_e + pl.program_id(1)
    n_steps = pl.num_programs(0) * per_e
    slot = s % 2
    cur = bufs.at[slot]
    prv = bufs.at[1 - slot]
    off_cur = (e % 2) * n_tok
    s_prev = (s + n_steps - 1) % n_steps
    off_prv = ((s_prev // per_e) % 2) * n_tok

    @pl.when(s == 0)
    def _():
        def start(r, c):
            _row_copy(acc_hbm, prv, off_prv + idxp_ref[0, 0, r], r, ssem).start()
            return c
        lax.fori_loop(0, tm, start, 0, unroll=8)

        def wait(r, c):
            _row_copy(acc_hbm, prv, off_prv + idxp_ref[0, 0, r], r, ssem).wait()
            return c
        lax.fori_loop(0, tm, wait, 0, unroll=8)

    eslot = e % 2

    def wd_copy(ee, sl):
        return pltpu.make_async_copy(wd_hbm.at[ee], wdbuf.at[sl], wsems.at[sl])

    @pl.when(pl.program_id(1) == 0)
    def _():
        @pl.when(e == 0)
        def _():
            wd_copy(e, eslot).start()

        @pl.when(e + 1 < pl.num_programs(0))
        def _():
            wd_copy(e + 1, 1 - eslot).start()

        wd_copy(e, eslot).wait()

    def step_body(wd_slot):
        for r in range(tm):
            _row_copy(acc_hbm, cur, off_cur + idx_ref[0, 0, r], r, gsem).start()
        for r in range(tm):
            _row_copy_back(prv, acc_hbm, off_prv + idxp_ref[0, 0, r], r, ssem).start()
        ye = jnp.dot(h_ref[...], wdbuf[wd_slot], preferred_element_type=F32) * g_ref[:, 0:1]
        for r in range(tm):
            _row_copy(acc_hbm, cur, off_cur + idx_ref[0, 0, r], r, gsem).wait()
        for r in range(tm):
            _row_copy_back(prv, acc_hbm, off_prv + idxp_ref[0, 0, r], r, ssem).wait()
        cur[...] = cur[...] + ye

    for wd_slot in range(2):
        pl.when(eslot == wd_slot)(functools.partial(step_body, wd_slot))

    @pl.when(s == n_steps - 1)
    def _():
        def start(r, c):
            _row_copy_back(cur, acc_hbm, off_cur + idx_ref[0, 0, r], r, ssem).start()
            return c
        lax.fori_loop(0, tm, start, 0, unroll=8)

        def wait(r, c):
            _row_copy_back(cur, acc_hbm, off_cur + idx_ref[0, 0, r], r, ssem).wait()
            return c
        lax.fori_loop(0, tm, wait, 0, unroll=8)


def _moe_down(idx3, h, w_down, g, acc, cap, tm):
    rows, ff = h.shape
    n_e, _, d = w_down.shape
    per_e = cap // tm
    n_steps = n_e * per_e
    n_tok = acc.shape[0] // 2
    blk = lambda e, i: e * per_e + i
    return pl.pallas_call(
        functools.partial(_moe_down_kernel, per_e, n_tok),
        grid=(n_e, per_e),
        in_specs=[pl.BlockSpec((1, 1, tm), lambda e, i: (blk(e, i), 0, 0), memory_space=pltpu.SMEM),
                  pl.BlockSpec((1, 1, tm), lambda e, i: ((blk(e, i) + n_steps - 1) % n_steps, 0, 0),
                               memory_space=pltpu.SMEM),
                  pl.BlockSpec((tm, ff), lambda e, i: (blk(e, i), 0)),
                  pl.BlockSpec(memory_space=pl.ANY),
                  pl.BlockSpec((tm, LANES), lambda e, i: (blk(e, i), 0)),
                  pl.BlockSpec(memory_space=pl.ANY)],
        out_specs=pl.BlockSpec(memory_space=pl.ANY),
        out_shape=jax.ShapeDtypeStruct(acc.shape, acc.dtype),
        scratch_shapes=[pltpu.VMEM((2, tm, d), F32), pltpu.SemaphoreType.DMA(()), pltpu.SemaphoreType.DMA(()),
                        pltpu.VMEM((2, ff, d), BF16), pltpu.SemaphoreType.DMA((2,))],
        input_output_aliases={5: 0},
        compiler_params=_cparams(("arbitrary", "arbitrary")),
        name="moe_down",
    )(idx3, idx3, h, w_down, g, acc)


def _ln2_kernel(x_ref, g_ref, b_ref, y_ref, ybf_ref):
    y = _layer_norm_rows(x_ref[0] + x_ref[1], g_ref[...], b_ref[...])
    y_ref[...] = y
    ybf_ref[...] = y.astype(ybf_ref.dtype)


def _ln2(x, g, b, tm):
    _, m, d = x.shape
    row = lambda i: (i, 0)
    return pl.pallas_call(
        _ln2_kernel,
        grid=(m // tm,),
        in_specs=[pl.BlockSpec((2, tm, d), lambda i: (0, i, 0)),
                  pl.BlockSpec((1, d), lambda i: (0, 0)),
                  pl.BlockSpec((1, d), lambda i: (0, 0))],
        out_specs=[pl.BlockSpec((tm, d), row), pl.BlockSpec((tm, d), row)],
        out_shape=[jax.ShapeDtypeStruct((m, d), F32), jax.ShapeDtypeStruct((m, d), BF16)],
        compiler_params=_cparams(("parallel",)),
        name="ln2",
    )(x, g, b)


def _rope_tables(seq):
    rows = seq // GRID_W
    row = jnp.repeat(jnp.arange(rows), GRID_W).astype(F32)
    col = (jnp.arange(rows * GRID_W) % GRID_W).astype(F32)
    inv = ROPE_THETA ** (-jnp.arange(ROPE_FREQS, dtype=F32) / ROPE_FREQS)
    ang_r = row[:, None] * inv
    ang_c = col[:, None] * inv
    cos_t = jnp.concatenate([jnp.cos(ang_r), jnp.cos(ang_r), jnp.cos(ang_c), jnp.cos(ang_c)], axis=1)
    sin_t = jnp.concatenate([-jnp.sin(ang_r), jnp.sin(ang_r), -jnp.sin(ang_c), jnp.sin(ang_c)], axis=1)
    return cos_t, sin_t


def _prep_layer(l, w_in, b_gate, q_norm_g, k_norm_g, sgu_ln_g, sgu_ln_b, w_s, b_s, w_mem_kv, w_branch,
                w_o, ln1_g, ln1_b, w_router, w_gate, w_up, w_down, ln2_g, ln2_b):
    d = w_in.shape[1]
    wi = w_in[l]
    c_q, c_k, c_v = ATTN_WIDTH, ATTN_WIDTH + KV_WIDTH, ATTN_WIDTH + 2 * KV_WIDTH
    c_u = c_v + SGU_WIDTH
    c_vb = c_u + SGU_WIDTH
    c_qm = c_vb + MEM_WIDTH
    wr = jnp.pad(w_router[l], ((0, 0), (0, LANES - N_EXPERTS)))
    wr_hi = wr.astype(BF16)
    wr_lo = (wr - wr_hi.astype(F32)).astype(BF16)
    return dict(
        w_attn=jnp.concatenate([wi[:, :c_v], wi[:, c_vb:c_qm]], axis=1).astype(BF16),
        w_uv=wi[:, c_v:c_vb].astype(BF16),
        w_g=wi[:, c_qm:].astype(BF16),
        b_g=b_gate[l].reshape(1, N_BRANCH * d),
        gq=q_norm_g[l].reshape(1, HEAD_DIM), gk=k_norm_g[l].reshape(1, HEAD_DIM),
        sgu_g=sgu_ln_g[l].reshape(1, SGU_WIDTH), sgu_b=sgu_ln_b[l].reshape(1, SGU_WIDTH),
        w_s=w_s[l].astype(BF16), b_s=b_s[l].reshape(SGU_GROUPS, CHUNK, 1),
        w_mem_kv=w_mem_kv[l].astype(BF16), w_branch=w_branch[l].astype(BF16), w_o=w_o[l].astype(BF16),
        ln1_g=ln1_g[l].reshape(1, d), ln1_b=ln1_b[l].reshape(1, d),
        wr=jnp.concatenate([wr_hi, wr_lo], axis=1),
        layer=l, w_gate=w_gate, w_up=w_up, w_down=w_down,
        ln2_g=ln2_g[l].reshape(1, d), ln2_b=ln2_b[l].reshape(1, d),
    )


def _run_trunk(x, mem, layers, alpha, wd_bf16):
    bsz, seq, d = x.shape
    n_mem = mem.shape[1]
    m = bsz * seq
    cap = EC_FACTOR * m // N_EXPERTS
    cos_t, sin_t = _rope_tables(seq)
    xf = x.reshape(m, d)
    x_bf = xf.astype(BF16)
    mem_bf = mem.reshape(bsz * n_mem, d).astype(BF16)
    tm_moe = min(512, cap)
    for p in layers:
        q, k, v, qm = _proj_attn(x_bf, p["w_attn"], cos_t, sin_t, p["gq"], p["gk"], seq, tm=512)
        out_b = _proj_sgu(x_bf, p["w_uv"], p["sgu_g"], p["sgu_b"], p["w_s"], p["b_s"], tm=512)
        out_a = _gqa(q, k, v, bsz, seq, tq=256)
        kv = _mm_plain(mem_bf, p["w_mem_kv"], tm=min(512, bsz * n_mem))
        out_c = _mem_attn(qm, kv, bsz, seq, n_mem, tq=512)
        merged = _branch_merge(x_bf, p["w_g"], p["b_g"], out_a, out_b, out_c, p["w_branch"], tm=512, tn=512)
        x1e, acc, aff_t = _out_ln1(merged, p["w_o"], xf, p["ln1_g"], p["ln1_b"], p["wr"], alpha, tm=512)
        idx = _route(aff_t.reshape(N_EXPERTS, m // LANES, LANES), cap)
        idx3 = idx.reshape(N_EXPERTS * cap // tm_moe, 1, tm_moe)
        xe, g = _moe_gather(idx3, x1e, cap, tm_moe)
        cast_wd = p["layer"] not in wd_bf16
        res = _moe_up(xe, p["w_gate"], p["w_up"], p["w_down"] if cast_wd else None, p["layer"], cap,
                      tm=min(1024, cap), tf=min(512, p["w_gate"].shape[3]))
        if cast_wd:
            h, wd_bf16[p["layer"]] = res
        else:
            h = res
        acc = _moe_down(idx3, h, wd_bf16[p["layer"]], g, acc.reshape(2 * m, d), cap, tm_moe)
        xf, x_bf = _ln2(acc.reshape(2, m, d), p["ln2_g"], p["ln2_b"], tm=512)
    return xf.reshape(bsz, seq, d)


def kernel(x_prompt, x_sample, mem_prompt, mem_sample, w_in, b_gate, q_norm_g, k_norm_g, sgu_ln_g, sgu_ln_b,
           w_s, b_s, w_mem_kv, w_branch, w_o, ln1_g, ln1_b, w_router, w_gate, w_up, w_down, ln2_g, ln2_b):
    depth = w_in.shape[0]
    alpha = (2 * depth) ** 0.25
    layers = [_prep_layer(l, w_in, b_gate, q_norm_g, k_norm_g, sgu_ln_g, sgu_ln_b, w_s, b_s, w_mem_kv,
                          w_branch, w_o, ln1_g, ln1_b, w_router, w_gate, w_up, w_down, ln2_g, ln2_b)
              for l in range(depth)]
    wd_bf16 = {}
    y_prompt = _run_trunk(x_prompt, mem_prompt, layers, alpha, wd_bf16)
    y_sample = _run_trunk(x_sample, mem_sample, layers, alpha, wd_bf16)
    return (y_prompt, y_sample)
```
